```python
import functools
import jax
import jax.numpy as jnp
from jax import lax
import numpy as np

D_MODEL = 2048
BATCH = 8
SEQ = 4096
DEPTH = 4
DEC_BATCH = 8
DEC_SEQ = 16
PAST_LEN = 4096

CHUNK = 64
Q_BLOCK = 128
FOX_HEADS = 8
FOX_HEAD_DIM = 128
FOX_WIDTH = FOX_HEADS * FOX_HEAD_DIM
GDN_HEADS = 8
GDN_DK = 128
GDN_DV = 128
GDN_KEY_WIDTH = GDN_HEADS * GDN_DK
GDN_VAL_WIDTH = GDN_HEADS * GDN_DV
GDN_CONV_CH = 2 * GDN_KEY_WIDTH + GDN_VAL_WIDTH
CONV_W = 4
PLE_DIM = 256
LN_EPS = 1e-5
RMS_EPS = 1e-6
FOX_SCALE = FOX_HEAD_DIM ** -0.5
GDN_SCALE = GDN_DK ** -0.5
DEEPNORM_ALPHA = (2 * DEPTH) ** 0.25
DEEPNORM_BETA = (8 * DEPTH) ** -0.25
SEGMENTS = (FOX_WIDTH, FOX_WIDTH, FOX_WIDTH, FOX_HEADS, FOX_WIDTH,
            GDN_CONV_CH, GDN_HEADS, GDN_HEADS, GDN_VAL_WIDTH,
            D_MODEL, D_MODEL)
SPLIT_IDX = tuple(int(i) for i in np.cumsum(SEGMENTS)[:-1])
N_IN = int(sum(SEGMENTS))

kernel_name = 'fox_gdn_parallel_streaming_encoder'


def layer_norm(x, g, b):
    xf = x.astype(jnp.float32)
    xc = xf - jnp.mean(xf, axis=-1, keepdims=True)
    var = jnp.mean(xc * xc, axis=-1, keepdims=True)
    return (xc * lax.rsqrt(var + LN_EPS) * g + b).astype(x.dtype)


def rms_norm(x, w):
    xf = x.astype(jnp.float32)
    return (xf * lax.rsqrt(jnp.mean(xf * xf, axis=-1, keepdims=True) + RMS_EPS) * w).astype(x.dtype)


def l2_normalize(x):
    xf = x.astype(jnp.float32)
    return xf * lax.rsqrt(jnp.sum(xf * xf, axis=-1, keepdims=True) + RMS_EPS)


def fox_attend(q, k, v, c_q, c_k, q_pos, k_pos):
    s = jnp.einsum('bqhd,bkhd->bhqk', q, k, preferred_element_type=jnp.float32) * FOX_SCALE
    bias = jnp.swapaxes(c_q, 1, 2)[:, :, :, None] - jnp.swapaxes(c_k, 1, 2)[:, :, None, :]
    visible = k_pos[None, :] <= q_pos[:, None]
    p = jax.nn.softmax(jnp.where(visible, s + bias, -jnp.inf), axis=-1)
    return jnp.einsum('bhqk,bkhd->bqhd', p.astype(v.dtype), v)


def gated_delta_chunked(q, k, v, g, beta, s0):
    bsz, n_tok, h = q.shape[:3]
    dv = v.shape[-1]
    c = min(CHUNK, n_tok)
    n = n_tok // c

    def to_chunks(t):
        t = t.reshape((bsz, n, c, h) + t.shape[3:])
        return jnp.moveaxis(t, (1, 3), (0, 2))

    qc, kc, vc, bc = to_chunks(q), to_chunks(k), to_chunks(v), to_chunks(beta)
    gc = jnp.cumsum(to_chunks(g), axis=-1)
    idx = jnp.arange(c)
    incl = idx[:, None] >= idx[None, :]
    strict = idx[:, None] > idx[None, :]
    decay = jnp.exp(jnp.where(incl, gc[..., :, None] - gc[..., None, :], -jnp.inf))
    kb = kc * bc[..., None]
    lmat = jnp.where(strict, jnp.einsum('nbhid,nbhjd->nbhij', kb, kc) * decay, 0.0)
    amat = lmat + jnp.eye(c, dtype=lmat.dtype)
    solve = functools.partial(lax.linalg.triangular_solve, left_side=True, lower=True)
    u = solve(amat, vc * bc[..., None])
    w = solve(amat, kb * jnp.exp(gc)[..., None])
    qk = jnp.where(incl, jnp.einsum('nbhid,nbhjd->nbhij', qc, kc) * decay, 0.0)
    q_dec = qc * jnp.exp(gc)[..., None]
    k_tail = kc * jnp.exp(gc[..., -1:] - gc)[..., None]
    g_last = jnp.exp(gc[..., -1])

    def step(s, xs):
        q_n, w_n, u_n, qk_n, kt_n, gl_n = xs
        v_new = u_n - jnp.einsum('bhcd,bhde->bhce', w_n, s)
        o = jnp.einsum('bhcd,bhde->bhce', q_n, s) + jnp.einsum('bhij,bhje->bhie', qk_n, v_new)
        s = s * gl_n[..., None, None] + jnp.einsum('bhcd,bhce->bhde', kt_n, v_new)
        return s, o

    s_final, o = lax.scan(step, s0, (q_dec, w, u, qk, k_tail, g_last))
    o = jnp.moveaxis(o, (0, 2), (1, 3)).reshape(bsz, n_tok, h, dv)
    return o, s_final


def layer(x, p_l, w_in, fox_f_bias, gdn_conv_w, gdn_a_log, gdn_dt_bias, gdn_norm_w,
          w_out_fox, w_out_gdn, w_out, ln_g, ln_b, w_pl_proj, w_pl_gate, pl_norm_w,
          fox_past, conv_buf, s0):
    bsz, n_tok = x.shape[:2]
    z = x @ w_in
    q_a, k_a, v_a, f_a, gate_a, qkv_b, a_b, beta_b, gate_b, m_a, m_b = jnp.split(z, SPLIT_IDX, axis=-1)

    q_a = q_a.reshape(bsz, n_tok, FOX_HEADS, FOX_HEAD_DIM)
    k_a = k_a.reshape(bsz, n_tok, FOX_HEADS, FOX_HEAD_DIM)
    v_a = v_a.reshape(bsz, n_tok, FOX_HEADS, FOX_HEAD_DIM)
    logf = jax.nn.log_sigmoid(f_a.astype(jnp.float32) + fox_f_bias)
    if fox_past is None:
        c = jnp.cumsum(logf, axis=1)
        pos = jnp.arange(n_tok)
        nb = n_tok // Q_BLOCK
        qb = jnp.moveaxis(q_a.reshape(bsz, nb, Q_BLOCK, FOX_HEADS, FOX_HEAD_DIM), 1, 0)
        cb = jnp.moveaxis(c.reshape(bsz, nb, Q_BLOCK, FOX_HEADS), 1, 0)
        pb = pos.reshape(nb, Q_BLOCK)
        o_a = lax.map(lambda t: fox_attend(t[0], k_a, v_a, t[1], c, t[2], pos), (qb, cb, pb))
        o_a = jnp.moveaxis(o_a, 0, 1).reshape(bsz, n_tok, FOX_WIDTH)
    else:
        k_past, v_past, logf_past = fox_past
        n_past = k_past.shape[1]
        k_all = jnp.concatenate([k_past.astype(k_a.dtype), k_a], axis=1)
        v_all = jnp.concatenate([v_past.astype(v_a.dtype), v_a], axis=1)
        c = jnp.cumsum(jnp.concatenate([logf_past.astype(jnp.float32), logf], axis=1), axis=1)
        k_pos = jnp.arange(n_past + n_tok)
        o_a = fox_attend(q_a, k_all, v_all, c[:, n_past:], c, k_pos[n_past:], k_pos)
        o_a = o_a.reshape(bsz, n_tok, FOX_WIDTH)
    y_a = (o_a * jax.nn.silu(gate_a)) @ w_out_fox

    xpad = jnp.concatenate([conv_buf.astype(qkv_b.dtype), qkv_b], axis=1)
    new_conv = xpad[:, -(CONV_W - 1):]
    conv = jax.nn.silu(sum(xpad[:, i:i + n_tok] * gdn_conv_w[i] for i in range(CONV_W)))
    q_b, k_b, v_b = jnp.split(conv, [GDN_KEY_WIDTH, 2 * GDN_KEY_WIDTH], axis=-1)
    q_b = l2_normalize(q_b.reshape(bsz, n_tok, GDN_HEADS, GDN_DK)) * GDN_SCALE
    k_b = l2_normalize(k_b.reshape(bsz, n_tok, GDN_HEADS, GDN_DK))
    v_b = v_b.reshape(bsz, n_tok, GDN_HEADS, GDN_DV).astype(jnp.float32)
    g = -jnp.exp(gdn_a_log.astype(jnp.float32)) * jax.nn.softplus(a_b.astype(jnp.float32) + gdn_dt_bias)
    beta = jax.nn.sigmoid(beta_b.astype(jnp.float32))
    o_b, s_new = gated_delta_chunked(q_b, k_b, v_b, g, beta, s0.astype(jnp.float32))
    o_b = rms_norm(o_b, gdn_norm_w).astype(x.dtype).reshape(bsz, n_tok, GDN_VAL_WIDTH)
    y_b = (o_b * jax.nn.silu(gate_b)) @ w_out_gdn

    h = (jax.nn.sigmoid(m_a) * y_a + jax.nn.sigmoid(m_b) * y_b) @ w_out
    x1 = layer_norm(DEEPNORM_ALPHA * x + h, ln_g, ln_b)
    e = rms_norm(p_l @ w_pl_proj, pl_norm_w)
    x_out = x1 + jax.nn.sigmoid(x1 @ w_pl_gate) * e
    return (x_out, k_a, v_a, logf.astype(x.dtype), new_conv, s_new.astype(x.dtype))


def setup_inputs(seed: int = 0) -> dict:
    key = jax.random.key(seed)
    ks = jax.random.split(key, 24)
    nrm = jax.random.normal
    f32 = jnp.float32
    col_scale = jnp.concatenate([
        jnp.ones((2 * FOX_WIDTH,), f32), jnp.full((FOX_WIDTH,), DEEPNORM_BETA, f32),
        jnp.ones((FOX_HEADS + FOX_WIDTH + 2 * GDN_KEY_WIDTH,), f32),
        jnp.full((GDN_VAL_WIDTH,), DEEPNORM_BETA, f32),
        jnp.ones((2 * GDN_HEADS + GDN_VAL_WIDTH + 2 * D_MODEL,), f32)])
    dt = jnp.exp(jax.random.uniform(ks[12], (DEPTH, GDN_HEADS), f32, np.log(1e-3), np.log(1e-1)))
    return {
        'x_prompt': nrm(ks[0], (BATCH, SEQ, D_MODEL), f32),
        'x_sample': nrm(ks[1], (DEC_BATCH, DEC_SEQ, D_MODEL), f32),
        'cache_fox_k': nrm(ks[2], (DEPTH, DEC_BATCH, PAST_LEN, FOX_HEADS, FOX_HEAD_DIM), f32),
        'cache_fox_v': nrm(ks[3], (DEPTH, DEC_BATCH, PAST_LEN, FOX_HEADS, FOX_HEAD_DIM), f32),
        'cache_fox_logf': jax.nn.log_sigmoid(3.0 + nrm(ks[4], (DEPTH, DEC_BATCH, PAST_LEN, FOX_HEADS), f32)),
        'state_gdn_conv': nrm(ks[5], (DEPTH, DEC_BATCH, CONV_W - 1, GDN_CONV_CH), f32),
        'state_gdn': 0.1 * nrm(ks[6], (DEPTH, DEC_BATCH, GDN_HEADS, GDN_DK, GDN_DV), f32),
        'p_prompt': nrm(ks[7], (DEPTH, BATCH, SEQ, PLE_DIM), f32),
        'p_sample': nrm(ks[8], (DEPTH, DEC_BATCH, DEC_SEQ, PLE_DIM), f32),
        'w_in': nrm(ks[9], (DEPTH, D_MODEL, N_IN), f32) * (D_MODEL ** -0.5) * col_scale,
        'fox_f_bias': 3.0 + 0.5 * nrm(ks[10], (DEPTH, FOX_HEADS), f32),
        'gdn_conv_w': nrm(ks[11], (DEPTH, CONV_W, GDN_CONV_CH), f32) * (CONV_W ** -0.5),
        'gdn_a_log': jnp.log(jax.random.uniform(ks[13], (DEPTH, GDN_HEADS), f32, 1.0, 16.0)),
        'gdn_dt_bias': dt + jnp.log(-jnp.expm1(-dt)),
        'gdn_norm_w': 1.0 + 0.05 * nrm(ks[14], (DEPTH, GDN_DV), f32),
        'w_out_fox': nrm(ks[15], (DEPTH, FOX_WIDTH, D_MODEL), f32) * (FOX_WIDTH ** -0.5) * DEEPNORM_BETA,
        'w_out_gdn': nrm(ks[16], (DEPTH, GDN_VAL_WIDTH, D_MODEL), f32) * (GDN_VAL_WIDTH ** -0.5) * DEEPNORM_BETA,
        'w_out': nrm(ks[17], (DEPTH, D_MODEL, D_MODEL), f32) * (D_MODEL ** -0.5) * DEEPNORM_BETA,
        'ln_g': 1.0 + 0.05 * nrm(ks[18], (DEPTH, D_MODEL), f32),
        'ln_b': 0.02 * nrm(ks[19], (DEPTH, D_MODEL), f32),
        'w_pl_proj': nrm(ks[20], (DEPTH, PLE_DIM, D_MODEL), f32) * (PLE_DIM ** -0.5),
        'w_pl_gate': nrm(ks[21], (DEPTH, D_MODEL, D_MODEL), f32) * (D_MODEL ** -0.5),
        'pl_norm_w': 1.0 + 0.05 * nrm(ks[22], (DEPTH, D_MODEL), f32),
    }


def reference(x_prompt, x_sample, cache_fox_k, cache_fox_v, cache_fox_logf, state_gdn_conv, state_gdn,
              p_prompt, p_sample, w_in, fox_f_bias, gdn_conv_w, gdn_a_log, gdn_dt_bias, gdn_norm_w,
              w_out_fox, w_out_gdn, w_out, ln_g, ln_b, w_pl_proj, w_pl_gate, pl_norm_w):
    xp, xs = x_prompt, x_sample
    bp = x_prompt.shape[0]
    kp_l, vp_l, lfp_l, cvp_l, sp_l = [], [], [], [], []
    ks_l, vs_l, lfs_l, cvs_l, ss_l = [], [], [], [], []
    for i in range(DEPTH):
        w = (w_in[i], fox_f_bias[i], gdn_conv_w[i], gdn_a_log[i], gdn_dt_bias[i], gdn_norm_w[i],
             w_out_fox[i], w_out_gdn[i], w_out[i], ln_g[i], ln_b[i], w_pl_proj[i], w_pl_gate[i], pl_norm_w[i])
        conv0 = jnp.zeros((bp, CONV_W - 1, GDN_CONV_CH), xp.dtype)
        s00 = jnp.zeros((bp, GDN_HEADS, GDN_DK, GDN_DV), xp.dtype)
        xp, kp, vp, lfp, cvp, sp = layer(xp, p_prompt[i], *w, None, conv0, s00)
        xs, kss, vss, lfs, cvs, sss = layer(xs, p_sample[i], *w,
                                            (cache_fox_k[i], cache_fox_v[i], cache_fox_logf[i]),
                                            state_gdn_conv[i], state_gdn[i])
        kp_l.append(kp); vp_l.append(vp); lfp_l.append(lfp); cvp_l.append(cvp); sp_l.append(sp)
        ks_l.append(kss); vs_l.append(vss); lfs_l.append(lfs); cvs_l.append(cvs); ss_l.append(sss)
    return (xp, xs,
            jnp.stack(kp_l), jnp.stack(vp_l), jnp.stack(lfp_l), jnp.stack(cvp_l), jnp.stack(sp_l),
            jnp.stack(ks_l), jnp.stack(vs_l), jnp.stack(lfs_l), jnp.stack(cvs_l), jnp.stack(ss_l))
```

```python
import functools
import math

import jax
import jax.numpy as jnp
from jax import lax
from jax.experimental import pallas as pl
from jax.experimental.pallas import tpu as pltpu

F32 = jnp.float32
BF16 = jnp.bfloat16

HEADS = 8
HEAD_DIM = 128
WIDTH = HEADS * HEAD_DIM
CONV_W = 4
CONV_CH = 3 * WIDTH
LN_EPS = 1e-5
RMS_EPS = 1e-6
LANES = 128
SUBLANES = 8
GDN_CHUNK = 128
INV_BASE = 8
VMEM_LIMIT = 56 * 1024 * 1024

Z_Q, Z_K, Z_V, Z_GATE_A, Z_QKV_B, Z_GATE_B, Z_M_A, Z_M_B = 0, 1, 2, 3, 4, 7, 8, 10


def _params(*sem):
    return pltpu.CompilerParams(dimension_semantics=sem, vmem_limit_bytes=VMEM_LIMIT)


def _mm(a, b):
    return jnp.dot(a.astype(BF16), b.astype(BF16), preferred_element_type=F32)


def _mm_nt(a, b):
    return lax.dot_general(a.astype(BF16), b.astype(BF16), (((1,), (1,)), ((), ())),
                           preferred_element_type=F32)


def _mm_tn(a, b):
    return lax.dot_general(a.astype(BF16), b.astype(BF16), (((0,), (0,)), ((), ())),
                           preferred_element_type=F32)


def _sigmoid(x):
    return 1.0 / (1.0 + jnp.exp(-x))


def _silu(x):
    return x * _sigmoid(x)


def _proj_kernel(x_ref, w_ref, ws_ref, par_ref, z_ref, s_ref, xb_ref):
    @pl.when(pl.program_id(1) == 0)
    def _():
        x = x_ref[...]
        xb = x.astype(BF16)
        xb_ref[...] = xb
        x_lo = (x - xb.astype(F32)).astype(BF16)
        raw = (jnp.dot(xb, ws_ref[0], preferred_element_type=F32)
               + jnp.dot(x_lo, ws_ref[0], preferred_element_type=F32)
               + jnp.dot(xb, ws_ref[1], preferred_element_type=F32))
        y = raw + par_ref[0:1, :]
        neg_a = -jnp.exp(par_ref[1:2, :])
        t = jnp.log1p(jnp.exp(-jnp.abs(y)))
        logf = jnp.minimum(y, 0.0) - t
        g = neg_a * (jnp.maximum(y, 0.0) + t)
        beta = _sigmoid(raw)
        lane = lax.broadcasted_iota(jnp.int32, raw.shape, 1)
        s_ref[...] = jnp.where(lane < HEADS, logf,
                               jnp.where(lane < 2 * HEADS, g,
                                         jnp.where(lane < 3 * HEADS, beta, 0.0)))

    z_ref[...] = jnp.dot(xb_ref[...], w_ref[...], preferred_element_type=F32)


def _proj(x2d, w_big, w_small, par):
    t_rows, d = x2d.shape
    n_big = w_big.shape[1]
    tm = min(1024, t_rows)
    tn = 512
    return pl.pallas_call(
        _proj_kernel,
        grid=(t_rows // tm, n_big // tn),
        in_specs=[
            pl.BlockSpec((tm, d), lambda i, j: (i, 0)),
            pl.BlockSpec((d, tn), lambda i, j: (0, j)),
            pl.BlockSpec((2, d, LANES), lambda i, j: (0, 0, 0)),
            pl.BlockSpec((SUBLANES, LANES), lambda i, j: (0, 0)),
        ],
        out_specs=[
            pl.BlockSpec((tm, tn), lambda i, j: (i, j)),
            pl.BlockSpec((tm, LANES), lambda i, j: (i, 0)),
        ],
        out_shape=[
            jax.ShapeDtypeStruct((t_rows, n_big), F32),
            jax.ShapeDtypeStruct((t_rows, LANES), F32),
        ],
        scratch_shapes=[pltpu.VMEM((tm, d), BF16)],
        compiler_params=_params("arbitrary", "arbitrary"),
        name="proj",
    )(x2d, w_big, w_small, par)


def _cumsum_kernel(x_ref, o_ref, carry_ref, *, blk, period):
    r = lax.broadcasted_iota(jnp.int32, (blk, blk), 0)
    c = lax.broadcasted_iota(jnp.int32, (blk, blk), 1)
    keep = r <= c
    if period is not None:
        keep = keep & ((r // period) == (c // period))
    y = jnp.dot(x_ref[...], keep.astype(F32), precision=lax.Precision.HIGHEST,
                preferred_element_type=F32)
    if period is None:
        @pl.when(pl.program_id(0) == 0)
        def _():
            carry_ref[...] = jnp.zeros_like(carry_ref)

        y = y + carry_ref[...]
        carry_ref[...] = y[:, blk - 1:blk]
    o_ref[...] = y


def _cumsum_rows(x, period=None):
    rows, length = x.shape
    blk = min(512, length)
    assert length % blk == 0 and (period is None or blk % period == 0)
    return pl.pallas_call(
        functools.partial(_cumsum_kernel, blk=blk, period=period),
        grid=(length // blk,),
        in_specs=[pl.BlockSpec((rows, blk), lambda j: (0, j))],
        out_specs=pl.BlockSpec((rows, blk), lambda j: (0, j)),
        out_shape=jax.ShapeDtypeStruct((rows, length), F32),
        scratch_shapes=[pltpu.VMEM((rows, 1), F32)],
        compiler_params=_params("arbitrary"),
        name="cumsum",
    )(x)


def _fox_kernel(q_ref, k_ref, v_ref, c_ref, ga_ref, o_ref, kb_ref, vb_ref, *, bq, scale):
    qi = pl.program_id(2)

    @pl.when(qi == 0)
    def _():
        kb_ref[...] = k_ref[...].astype(BF16)
        vb_ref[...] = v_ref[...].astype(BF16)

    q = (q_ref[...] * scale).astype(BF16)
    row = lax.broadcasted_iota(jnp.int32, (bq, bq), 0)
    col = lax.broadcasted_iota(jnp.int32, (bq, bq), 1)

    def block(kj, carry, masked):
        m, l, acc = carry
        start = pl.multiple_of(kj * bq, bq)
        s = lax.dot_general(q, kb_ref[pl.ds(start, bq), :], (((1,), (1,)), ((), ())),
                            preferred_element_type=F32)
        t = s - c_ref[kj]
        if masked:
            t = jnp.where(row >= col, t, -jnp.inf)
        m_new = jnp.maximum(m, jnp.max(t, axis=1, keepdims=True))
        alpha = jnp.exp(m - m_new)
        p = jnp.exp(t - m_new)
        l = alpha * l + jnp.sum(p, axis=1, keepdims=True)
        acc = alpha * acc + jnp.dot(p.astype(BF16), vb_ref[pl.ds(start, bq), :],
                                    preferred_element_type=F32)
        return m_new, l, acc

    init = (jnp.full((bq, 1), -jnp.inf, F32), jnp.zeros((bq, 1), F32),
            jnp.zeros((bq, HEAD_DIM), F32))
    carry = lax.fori_loop(0, qi, lambda kj, cr: block(kj, cr, False), init)
    _, l, acc = block(qi, carry, True)
    o_ref[...] = (acc / l * _silu(ga_ref[...])).astype(BF16)


def _fox_prompt(z3, c_rows):
    bsz, length, _ = z3.shape
    bq = min(512, length)
    nk = length // bq
    c5 = c_rows.reshape(bsz, HEADS, nk, 1, bq)
    hpw = WIDTH // HEAD_DIM
    return pl.pallas_call(
        functools.partial(_fox_kernel, bq=bq, scale=HEAD_DIM ** -0.5),
        grid=(bsz, HEADS, nk),
        in_specs=[
            pl.BlockSpec((None, bq, HEAD_DIM), lambda b, h, i: (b, i, Z_Q * hpw + h)),
            pl.BlockSpec((None, length, HEAD_DIM), lambda b, h, i: (b, 0, Z_K * hpw + h)),
            pl.BlockSpec((None, length, HEAD_DIM), lambda b, h, i: (b, 0, Z_V * hpw + h)),
            pl.BlockSpec((None, None, nk, 1, bq), lambda b, h, i: (b, h, 0, 0, 0)),
            pl.BlockSpec((None, bq, HEAD_DIM), lambda b, h, i: (b, i, Z_GATE_A * hpw + h)),
        ],
        out_specs=pl.BlockSpec((None, bq, HEAD_DIM), lambda b, h, i: (b, i, h)),
        out_shape=jax.ShapeDtypeStruct((bsz, length, WIDTH), BF16),
        scratch_shapes=[pltpu.VMEM((length, HEAD_DIM), BF16), pltpu.VMEM((length, HEAD_DIM), BF16)],
        compiler_params=_params("arbitrary", "arbitrary", "arbitrary"),
        name="fox_prompt",
    )(z3, z3, z3, c5, z3)


def _fox_decode_kernel(q_ref, kn_ref, vn_ref, kp_ref, vp_ref, cp_ref, cn_ref, ga_ref, o_ref, *, scale):
    n = q_ref.shape[0]
    past = kp_ref.shape[0]
    q = (q_ref[...] * scale).astype(BF16)
    cp = cp_ref[...]
    t_past = _mm_nt(q, kp_ref[...]) - (cp - cp[:, past - 1:past])
    row = lax.broadcasted_iota(jnp.int32, (n, n), 0)
    col = lax.broadcasted_iota(jnp.int32, (n, n), 1)
    t_new = jnp.where(row >= col, _mm_nt(q, kn_ref[...]) - cn_ref[...], -jnp.inf)
    m = jnp.maximum(jnp.max(t_past, axis=1, keepdims=True), jnp.max(t_new, axis=1, keepdims=True))
    p_past = jnp.exp(t_past - m)
    p_new = jnp.exp(t_new - m)
    l = jnp.sum(p_past, axis=1, keepdims=True) + jnp.sum(p_new, axis=1, keepdims=True)
    acc = _mm(p_past, vp_ref[...]) + _mm(p_new, vn_ref[...])
    o_ref[...] = (acc / l * _silu(ga_ref[...])).astype(BF16)


def _fox_decode(z3, n_tok, cache_k, cache_v, layer, cp_rows, cn_rows):
    bsz = z3.shape[0]
    past = cache_k.shape[2]
    hpw = WIDTH // HEAD_DIM
    cp5 = cp_rows.reshape(-1, bsz, HEADS, 1, past)
    cn4 = cn_rows.reshape(bsz, HEADS, 1, n_tok)
    zblk = lambda seg: pl.BlockSpec((None, n_tok, HEAD_DIM), lambda b, h: (b, 0, seg * hpw + h))
    cache_k = cache_k.reshape(cache_k.shape[:3] + (WIDTH,))
    cache_v = cache_v.reshape(cache_v.shape[:3] + (WIDTH,))
    cblk = pl.BlockSpec((None, None, past, HEAD_DIM), lambda b, h: (layer, b, 0, h))
    return pl.pallas_call(
        functools.partial(_fox_decode_kernel, scale=HEAD_DIM ** -0.5),
        grid=(bsz, HEADS),
        in_specs=[
            zblk(Z_Q), zblk(Z_K), zblk(Z_V), cblk, cblk,
            pl.BlockSpec((None, None, None, 1, past), lambda b, h: (layer, b, h, 0, 0)),
            pl.BlockSpec((None, None, 1, n_tok), lambda b, h: (b, h, 0, 0)),
            zblk(Z_GATE_A),
        ],
        out_specs=pl.BlockSpec((None, n_tok, HEAD_DIM), lambda b, h: (b, 0, h)),
        out_shape=jax.ShapeDtypeStruct((bsz, n_tok, WIDTH), BF16),
        compiler_params=_params("arbitrary", "arbitrary"),
        name="fox_decode",
    )(z3, z3, z3, cache_k, cache_v, cp5, cn4, z3)


def _gdn_kernel(qp_ref, kp_ref, vp_ref, cw_ref, c0_ref, cols_ref, gr_ref, gb_ref, nw_ref, s0_ref,
                og_ref, s_ref, xp_ref, *, chunk, scale):
    pad = SUBLANES

    @pl.when(pl.program_id(1) == 0)
    def _():
        s_ref[...] = s0_ref[...]
        xp_ref[0:pad, :] = c0_ref[...]

    xp_ref[pad:pad + chunk, 0:WIDTH] = qp_ref[...]
    xp_ref[pad:pad + chunk, WIDTH:2 * WIDTH] = kp_ref[...]
    xp_ref[pad:pad + chunk, 2 * WIDTH:3 * WIDTH] = vp_ref[...]
    conv = xp_ref[pl.ds(pad - CONV_W + 1, chunk), :] * cw_ref[0:1, :]
    for i in range(1, CONV_W):
        conv = conv + xp_ref[pl.ds(pad - CONV_W + 1 + i, chunk), :] * cw_ref[i:i + 1, :]
    conv = _silu(conv)
    xp_ref[0:pad, :] = xp_ref[chunk:chunk + pad, :]

    cols = cols_ref[...]
    row = lax.broadcasted_iota(jnp.int32, (chunk, chunk), 0)
    col = lax.broadcasted_iota(jnp.int32, (chunk, chunk), 1)
    incl = row >= col
    strict = row > col
    sizes = [INV_BASE << i for i in range(int(math.log2(chunk // INV_BASE)) + 1)]
    same = [(row >> int(math.log2(b))) == (col >> int(math.log2(b))) for b in sizes[:-1]]
    diag_mask = strict & same[0]
    off_masks = [strict & ~same[i] & (same[i + 1] if i + 1 < len(same) else True)
                 for i in range(len(same))]

    for h in range(HEADS):
        lo, hi = h * HEAD_DIM, (h + 1) * HEAD_DIM
        q = conv[:, lo:hi]
        k = conv[:, WIDTH + lo:WIDTH + hi]
        v = conv[:, 2 * WIDTH + lo:2 * WIDTH + hi]
        q = q * (lax.rsqrt(jnp.sum(q * q, axis=1, keepdims=True) + RMS_EPS) * scale)
        k = k * lax.rsqrt(jnp.sum(k * k, axis=1, keepdims=True) + RMS_EPS)
        gc = cols[:, h:h + 1]
        beta = cols[:, HEADS + h:HEADS + h + 1]
        decay = jnp.exp(jnp.where(incl, gc - gr_ref[h:h + 1, :], -jnp.inf))
        e = jnp.exp(gc)
        g_last = gc[chunk - 1:chunk, :]
        kb = k * beta
        lmat = jnp.where(strict, _mm_nt(kb, k) * decay, 0.0)
        qk = jnp.where(incl, _mm_nt(q, k) * decay, 0.0)
        xpow = jnp.where(diag_mask, lmat, 0.0)
        nm = -xpow
        for _ in range(int(math.log2(INV_BASE)) - 1):
            xpow = _mm(xpow, xpow)
            nm = nm + xpow + _mm(nm, xpow)
        for off in off_masks:
            loff = jnp.where(off, lmat, 0.0)
            t_loff = loff + _mm(nm, loff)
            nm = nm - (t_loff + _mm(t_loff, nm))
        rhs = jnp.concatenate([v * beta, kb * e], axis=1)
        uw = rhs + _mm(nm, rhs)
        u = uw[:, :HEAD_DIM]
        w = uw[:, HEAD_DIM:]
        s_old = s_ref[h]
        ws_qs = _mm(jnp.concatenate([w, q * e], axis=0), s_old)
        v_new = u - ws_qs[:chunk]
        o = ws_qs[chunk:] + _mm(qk, v_new)
        k_tail = k * jnp.exp(g_last - gc)
        s_ref[h] = s_old * jnp.exp(g_last) + _mm_tn(k_tail, v_new)
        o = o * lax.rsqrt(jnp.mean(o * o, axis=1, keepdims=True) + RMS_EPS) * nw_ref[...]
        og_ref[:, lo:hi] = (o * _silu(gb_ref[:, lo:hi])).astype(BF16)


def _gdn(z3, conv_w, conv0, cols, gc_rows, norm_w, s0):
    bsz, length, _ = z3.shape
    chunk = GDN_CHUNK
    assert length % chunk == 0
    nc = length // chunk
    gr = gc_rows.reshape(bsz, HEADS, nc, chunk).transpose(0, 2, 1, 3)
    c0 = jnp.pad(conv0, ((0, 0), (SUBLANES - (CONV_W - 1), 0), (0, 0)))
    zblk = lambda seg: pl.BlockSpec((None, chunk, WIDTH), lambda b, l: (b, l, seg))
    return pl.pallas_call(
        functools.partial(_gdn_kernel, chunk=chunk, scale=HEAD_DIM ** -0.5),
        grid=(bsz, nc),
        in_specs=[
            zblk(Z_QKV_B), zblk(Z_QKV_B + 1), zblk(Z_QKV_B + 2),
            pl.BlockSpec((CONV_W, CONV_CH), lambda b, l: (0, 0)),
            pl.BlockSpec((None, SUBLANES, CONV_CH), lambda b, l: (b, 0, 0)),
            pl.BlockSpec((None, chunk, LANES), lambda b, l: (b, l, 0)),
            pl.BlockSpec((None, None, HEADS, chunk), lambda b, l: (b, l, 0, 0)),
            zblk(Z_GATE_B),
            pl.BlockSpec((1, HEAD_DIM), lambda b, l: (0, 0)),
            pl.BlockSpec((None, HEADS, HEAD_DIM, HEAD_DIM), lambda b, l: (b, 0, 0, 0)),
        ],
        out_specs=[
            pl.BlockSpec((None, chunk, WIDTH), lambda b, l: (b, l, 0)),
            pl.BlockSpec((None, HEADS, HEAD_DIM, HEAD_DIM), lambda b, l: (b, 0, 0, 0)),
        ],
        out_shape=[
            jax.ShapeDtypeStruct((bsz, length, WIDTH), BF16),
            jax.ShapeDtypeStruct((bsz, HEADS, HEAD_DIM, HEAD_DIM), F32),
        ],
        scratch_shapes=[pltpu.VMEM((chunk + SUBLANES, CONV_CH), F32)],
        compiler_params=_params("arbitrary", "arbitrary"),
        name="gdn",
    )(z3, z3, z3, conv_w, c0, cols, gr, z3, norm_w.reshape(1, HEAD_DIM), s0)


def _out_kernel(oa_ref, ob_ref, ma_ref, mb_ref, x_ref, p_ref, wfox_ref, wgdn_ref, wout_ref,
                wplp_ref, wplg_ref, vec_ref, o_ref, *, alpha):
    ya = jnp.dot(oa_ref[...], wfox_ref[...], preferred_element_type=F32)
    yb = jnp.dot(ob_ref[...], wgdn_ref[...], preferred_element_type=F32)
    merged = _sigmoid(ma_ref[...]) * ya + _sigmoid(mb_ref[...]) * yb
    y = alpha * x_ref[...] + _mm(merged, wout_ref[...])
    yc = y - jnp.mean(y, axis=1, keepdims=True)
    var = jnp.mean(yc * yc, axis=1, keepdims=True)
    x1 = yc * lax.rsqrt(var + LN_EPS) * vec_ref[0:1, :] + vec_ref[1:2, :]
    e = _mm(p_ref[...], wplp_ref[...])
    e = e * lax.rsqrt(jnp.mean(e * e, axis=1, keepdims=True) + RMS_EPS) * vec_ref[2:3, :]
    o_ref[...] = x1 + _sigmoid(_mm(x1, wplg_ref[...])) * e


def _out_block(oa, ob, z2d, x2d, p_all, layer, wfox, wgdn, wout, wplp, wplg, vec, alpha):
    t_rows, d = x2d.shape
    ple = p_all.shape[-1]
    tm = min(256, t_rows)
    dpw = d // WIDTH
    assert d % WIDTH == 0 and Z_M_A % dpw == 0 and Z_M_B % dpw == 0
    const = lambda shape: pl.BlockSpec(shape, lambda r: (0,) * len(shape), pipeline_mode=pl.Buffered(1))
    return pl.pallas_call(
        functools.partial(_out_kernel, alpha=alpha),
        grid=(t_rows // tm,),
        in_specs=[
            pl.BlockSpec((tm, WIDTH), lambda r: (r, 0)),
            pl.BlockSpec((tm, WIDTH), lambda r: (r, 0)),
            pl.BlockSpec((tm, d), lambda r: (r, Z_M_A // dpw)),
            pl.BlockSpec((tm, d), lambda r: (r, Z_M_B // dpw)),
            pl.BlockSpec((tm, d), lambda r: (r, 0)),
            pl.BlockSpec((None, tm, ple), lambda r: (layer, r, 0)),
            const((WIDTH, d)), const((WIDTH, d)), const((d, d)), const((ple, d)), const((d, d)),
            const((SUBLANES, d)),
        ],
        out_specs=pl.BlockSpec((tm, d), lambda r: (r, 0)),
        out_shape=jax.ShapeDtypeStruct((t_rows, d), F32),
        compiler_params=_params("arbitrary"),
        name="out_block",
    )(oa, ob, z2d, z2d, x2d, p_all, wfox, wgdn, wout, wplp, wplg, vec)


def _layer(x2d, bsz, n_tok, p_all, layer, wts, conv0, s0, alpha, fox_cache=None):
    lp = x2d.shape[0] // bsz
    z2d, small = _proj(x2d, wts["w_big"], wts["w_small"], wts["par"])
    z3 = z2d.reshape(bsz, lp, -1)
    small3 = small.reshape(bsz, lp, LANES)
    logf_all = small3[:, :, 0:HEADS]
    logf = logf_all[:, :n_tok]
    valid = (jnp.arange(lp) < n_tok)[None, :, None]
    g = jnp.where(valid, small3[:, :, HEADS:2 * HEADS], 0.0)
    beta = jnp.where(valid, small3[:, :, 2 * HEADS:3 * HEADS], 0.0)

    to_rows = lambda a: a.transpose(0, 2, 1).reshape(bsz * HEADS, a.shape[1])
    c_rows = _cumsum_rows(to_rows(logf_all))
    gc_rows = _cumsum_rows(to_rows(g), period=GDN_CHUNK)
    gc = gc_rows.reshape(bsz, HEADS, lp).transpose(0, 2, 1)
    cols = jnp.concatenate([gc, beta, jnp.zeros((bsz, lp, LANES - 2 * HEADS), F32)], axis=-1)

    if fox_cache is None:
        oa = _fox_prompt(z3, c_rows)
    else:
        cache_k, cache_v, cp_rows = fox_cache
        oa = _fox_decode(z3, n_tok, cache_k, cache_v, layer, cp_rows, c_rows[:, :n_tok])
        oa = jnp.pad(oa, ((0, 0), (0, lp - n_tok), (0, 0)))
    ob, s_new = _gdn(z3, wts["conv_w"], conv0, cols, gc_rows, wts["norm_w"], s0)

    x_out = _out_block(oa.reshape(bsz * lp, WIDTH), ob.reshape(bsz * lp, WIDTH), z2d, x2d, p_all, layer,
                       wts["wfox"], wts["wgdn"], wts["wout"], wts["wplp"], wts["wplg"], wts["vec"], alpha)

    k_a = z3[:, :n_tok, Z_K * WIDTH:(Z_K + 1) * WIDTH].reshape(bsz, n_tok, HEADS, HEAD_DIM)
    v_a = z3[:, :n_tok, Z_V * WIDTH:(Z_V + 1) * WIDTH].reshape(bsz, n_tok, HEADS, HEAD_DIM)
    new_conv = z3[:, n_tok - (CONV_W - 1):n_tok, Z_QKV_B * WIDTH:Z_QKV_B * WIDTH + CONV_CH]
    return x_out, k_a, v_a, logf, new_conv, s_new


def kernel(x_prompt, x_sample, cache_fox_k, cache_fox_v, cache_fox_logf, state_gdn_conv, state_gdn, p_prompt, p_sample, w_in, fox_f_bias, gdn_conv_w, gdn_a_log, gdn_dt_bias, gdn_norm_w, w_out_fox, w_out_gdn, w_out, ln_g, ln_b, w_pl_proj, w_pl_gate, pl_norm_w):
    depth = w_in.shape[0]
    bp, seq, d = x_prompt.shape
    bs, dseq, _ = x_sample.shape
    past = cache_fox_k.shape[2]
    ple = p_prompt.shape[-1]
    assert d % WIDTH == 0 and dseq >= CONV_W - 1 and seq >= CONV_W - 1
    alpha = (2 * depth) ** 0.25

    o_f = 3 * WIDTH
    o_ga = o_f + HEADS
    o_qkv = o_ga + WIDTH
    o_a = o_qkv + CONV_CH
    o_gb = o_a + 2 * HEADS
    w_big = jnp.concatenate([w_in[..., :o_f], w_in[..., o_ga:o_a], w_in[..., o_gb:]], axis=-1).astype(BF16)
    w_small = jnp.concatenate([w_in[..., o_f:o_ga], w_in[..., o_a:o_gb],
                               jnp.zeros((depth, d, LANES - 3 * HEADS), F32)], axis=-1)
    w_small_hi = w_small.astype(BF16)
    w_small = jnp.stack([w_small_hi, (w_small - w_small_hi.astype(F32)).astype(BF16)], axis=1)
    zpad = jnp.zeros((depth, LANES - 2 * HEADS), F32)
    par = jnp.stack([jnp.concatenate([fox_f_bias, gdn_dt_bias, zpad], axis=-1),
                     jnp.concatenate([jnp.zeros_like(fox_f_bias), gdn_a_log, zpad], axis=-1)], axis=1)
    par = jnp.pad(par, ((0, 0), (0, SUBLANES - 2), (0, 0)))
    vec = jnp.pad(jnp.stack([ln_g, ln_b, pl_norm_w], axis=1), ((0, 0), (0, SUBLANES - 3), (0, 0)))
    wfox, wgdn, wout = w_out_fox.astype(BF16), w_out_gdn.astype(BF16), w_out.astype(BF16)
    wplp, wplg = w_pl_proj.astype(BF16), w_pl_gate.astype(BF16)

    def layer_weights(i):
        return dict(w_big=w_big[i], w_small=w_small[i], par=par[i], conv_w=gdn_conv_w[i],
                    norm_w=gdn_norm_w[i], wfox=wfox[i], wgdn=wgdn[i], wout=wout[i],
                    wplp=wplp[i], wplg=wplg[i], vec=vec[i])

    lp_s = -(-dseq // GDN_CHUNK) * GDN_CHUNK
    xs = jnp.pad(x_sample, ((0, 0), (0, lp_s - dseq), (0, 0))).reshape(bs * lp_s, d)
    ps_all = jnp.pad(p_sample, ((0, 0), (0, 0), (0, lp_s - dseq), (0, 0))).reshape(depth, bs * lp_s, ple)
    xp = x_prompt.reshape(bp * seq, d)
    pp_all = p_prompt.reshape(depth, bp * seq, ple)

    cp_rows = _cumsum_rows(cache_fox_logf.transpose(0, 1, 3, 2).reshape(depth * bs * HEADS, past))

    conv0_p = jnp.zeros((bp, CONV_W - 1, CONV_CH), F32)
    s0_p = jnp.zeros((bp, HEADS, HEAD_DIM, HEAD_DIM), F32)
    outs_p, outs_s = [], []
    for i in range(depth):
        wts = layer_weights(i)
        xp, *rest_p = _layer(xp, bp, seq, pp_all, i, wts, conv0_p, s0_p, alpha)
        xs, *rest_s = _layer(xs, bs, dseq, ps_all, i, wts, state_gdn_conv[i], state_gdn[i], alpha,
                             fox_cache=(cache_fox_k, cache_fox_v, cp_rows))
        outs_p.append(rest_p)
        outs_s.append(rest_s)
    stack = lambda outs, j: jnp.stack([o[j] for o in outs])
    y_prompt = xp.reshape(bp, seq, d)
    y_sample = xs.reshape(bs, lp_s, d)[:, :dseq]
    return (y_prompt, y_sample,
            stack(outs_p, 0), stack(outs_p, 1), stack(outs_p, 2), stack(outs_p, 3), stack(outs_p, 4),
            stack(outs_s, 0), stack(outs_s, 1), stack(outs_s, 2), stack(outs_s, 3), stack(outs_s, 4))
```

```python
import functools
import math

import jax
import jax.numpy as jnp
from jax import lax
from jax.experimental import pallas as pl
from jax.experimental.pallas import tpu as pltpu

F32 = jnp.float32
BF16 = jnp.bfloat16

HEADS = 8
HEAD_DIM = 128
WIDTH = HEADS * HEAD_DIM
CONV_W = 4
CONV_CH = 3 * WIDTH
LN_EPS = 1e-5
RMS_EPS = 1e-6
LOG2E = math.log2(math.e)
LANES = 128
SUBLANES = 8
GDN_CHUNK = 128
INV_BASE = 8
FOX_BQ = 1024
FOX_BK = 512
FOX_ROWS = 256
VMEM_LIMIT = 56 * 1024 * 1024

Z_Q, Z_GATE_A, Z_QKV_B, Z_GATE_B, Z_M_A, Z_M_B, Z_END = 0, 1, 2, 5, 6, 8, 10


def _params(*sem):
    return pltpu.CompilerParams(dimension_semantics=sem, vmem_limit_bytes=VMEM_LIMIT)


def _mm(a, b):
    return jnp.dot(a.astype(BF16), b.astype(BF16), preferred_element_type=F32)


def _mm_nt(a, b):
    return lax.dot_general(a.astype(BF16), b.astype(BF16), (((1,), (1,)), ((), ())),
                           preferred_element_type=F32)


def _mm_tn(a, b):
    return lax.dot_general(a.astype(BF16), b.astype(BF16), (((0,), (0,)), ((), ())),
                           preferred_element_type=F32)


def _sigmoid(x):
    return 1.0 / (1.0 + jnp.exp(-x))


def _silu(x):
    return x * _sigmoid(x)


def _proj_kernel(x_ref, w_ref, ws_ref, par_ref, k_in_ref, v_in_ref, z_ref, s_ref, k_ref, v_ref, xb_ref,
                 *, z_tiles, kv_tiles):
    del k_in_ref, v_in_ref
    j = pl.program_id(1)

    @pl.when(j == 0)
    def _():
        x = x_ref[...]
        xb = x.astype(BF16)
        xb_ref[...] = xb
        x_lo = (x - xb.astype(F32)).astype(BF16)
        raw = (jnp.dot(xb, ws_ref[0], preferred_element_type=F32)
               + jnp.dot(x_lo, ws_ref[0], preferred_element_type=F32)
               + jnp.dot(xb, ws_ref[1], preferred_element_type=F32))
        y = raw + par_ref[0:1, :]
        neg_a = -jnp.exp(par_ref[1:2, :])
        t = jnp.log1p(jnp.exp(-jnp.abs(y)))
        logf = jnp.minimum(y, 0.0) - t
        g = neg_a * (jnp.maximum(y, 0.0) + t)
        beta = _sigmoid(raw)
        lane = lax.broadcasted_iota(jnp.int32, raw.shape, 1)
        s_ref[...] = jnp.where(lane < HEADS, logf,
                               jnp.where(lane < 2 * HEADS, g,
                                         jnp.where(lane < 3 * HEADS, beta, 0.0)))

    @pl.when(j < z_tiles)
    def _():
        z_ref[...] = jnp.dot(xb_ref[...], w_ref[...], preferred_element_type=F32)

    @pl.when((j >= z_tiles) & (j < z_tiles + kv_tiles))
    def _():
        k_ref[...] = jnp.dot(xb_ref[...], w_ref[...], preferred_element_type=F32)

    @pl.when(j >= z_tiles + kv_tiles)
    def _():
        v_ref[...] = jnp.dot(xb_ref[...], w_ref[...], preferred_element_type=F32)


def _proj(x2d, w_big, w_small, par, k_all, v_all, layer):
    t_rows, d = x2d.shape
    tm = min(1024, t_rows)
    tn = 512
    kv_tiles = WIDTH // tn
    z_tiles = w_big.shape[1] // tn - 2 * kv_tiles
    clamp = lambda j, lo: jnp.clip(j - lo, 0, kv_tiles - 1)
    return pl.pallas_call(
        functools.partial(_proj_kernel, z_tiles=z_tiles, kv_tiles=kv_tiles),
        grid=(t_rows // tm, z_tiles + 2 * kv_tiles),
        in_specs=[
            pl.BlockSpec((tm, d), lambda i, j: (i, 0)),
            pl.BlockSpec((d, tn), lambda i, j: (0, j)),
            pl.BlockSpec((2, d, LANES), lambda i, j: (0, 0, 0)),
            pl.BlockSpec((SUBLANES, LANES), lambda i, j: (0, 0)),
            pl.BlockSpec(memory_space=pl.ANY),
            pl.BlockSpec(memory_space=pl.ANY),
        ],
        out_specs=[
            pl.BlockSpec((tm, tn), lambda i, j: (i, jnp.minimum(j, z_tiles - 1))),
            pl.BlockSpec((tm, LANES), lambda i, j: (i, 0)),
            pl.BlockSpec((None, tm, tn), lambda i, j: (layer, i, clamp(j, z_tiles))),
            pl.BlockSpec((None, tm, tn), lambda i, j: (layer, i, clamp(j, z_tiles + kv_tiles))),
        ],
        out_shape=[
            jax.ShapeDtypeStruct((t_rows, z_tiles * tn), F32),
            jax.ShapeDtypeStruct((t_rows, LANES), F32),
            jax.ShapeDtypeStruct(k_all.shape, F32),
            jax.ShapeDtypeStruct(v_all.shape, F32),
        ],
        input_output_aliases={4: 2, 5: 3},
        scratch_shapes=[pltpu.VMEM((tm, d), BF16)],
        compiler_params=_params("arbitrary", "arbitrary"),
        name="proj",
    )(x2d, w_big, w_small, par, k_all, v_all)


def _cumsum_kernel(x_ref, o_ref, carry_ref, *, blk, period):
    r = lax.broadcasted_iota(jnp.int32, (blk, blk), 0)
    c = lax.broadcasted_iota(jnp.int32, (blk, blk), 1)
    keep = r <= c
    if period is not None:
        keep = keep & ((r // period) == (c // period))
    y = jnp.dot(x_ref[...], keep.astype(F32), precision=lax.Precision.HIGHEST,
                preferred_element_type=F32)
    if period is None:
        @pl.when(pl.program_id(0) == 0)
        def _():
            carry_ref[...] = jnp.zeros_like(carry_ref)

        y = y + carry_ref[...]
        carry_ref[...] = y[:, blk - 1:blk]
    o_ref[...] = y


def _cumsum_rows(x, period=None):
    rows, length = x.shape
    blk = min(512, length)
    assert length % blk == 0 and (period is None or blk % period == 0)
    return pl.pallas_call(
        functools.partial(_cumsum_kernel, blk=blk, period=period),
        grid=(length // blk,),
        in_specs=[pl.BlockSpec((rows, blk), lambda j: (0, j))],
        out_specs=pl.BlockSpec((rows, blk), lambda j: (0, j)),
        out_shape=jax.ShapeDtypeStruct((rows, length), F32),
        scratch_shapes=[pltpu.VMEM((rows, 1), F32)],
        compiler_params=_params("arbitrary"),
        name="cumsum",
    )(x)


def _fox_kernel(q_ref, k_ref, v_ref, c_ref, ga_ref, o_ref, kb_ref, vb_ref, *, bq, bk, rows, scale):
    qi = pl.program_id(2)
    length = k_ref.shape[0]
    groups = bq // rows

    @pl.when(qi == 0)
    def _():
        kb_ref[...] = k_ref[...].astype(BF16)
        vb_ref[:, :HEAD_DIM] = v_ref[...].astype(BF16)
        vb_ref[:, HEAD_DIM:] = jnp.ones((length, LANES), BF16)

    q = (q_ref[...] * (scale * LOG2E)).astype(BF16)
    qs = [q[r * rows:(r + 1) * rows] for r in range(groups)]
    row = lax.broadcasted_iota(jnp.int32, (rows, bk), 0)
    col = lax.broadcasted_iota(jnp.int32, (rows, bk), 1)

    def update(kj, carries, diag_offset=None):
        start = pl.multiple_of(kj * bk, bk)
        kblk = kb_ref[pl.ds(start, bk), :]
        vblk = vb_ref[pl.ds(start, bk), :]
        ck = c_ref[kj] * LOG2E
        ts = {}
        for r in range(groups):
            if diag_offset is not None and diag_offset > r * rows + rows - 1:
                continue
            t = lax.dot_general(qs[r], kblk, (((1,), (1,)), ((), ())), preferred_element_type=F32) - ck
            if diag_offset is not None and diag_offset + bk - 1 > r * rows:
                t = jnp.where(col + diag_offset <= row + r * rows, t, -jnp.inf)
            ts[r] = t
        out = list(carries)
        for r, t in ts.items():
            m, accl = carries[r]
            m_new = jnp.maximum(m, jnp.broadcast_to(jnp.max(t, axis=1, keepdims=True), m.shape))
            alpha = jnp.exp2(m - m_new)
            p = jnp.exp2(t - jnp.concatenate([m_new] * (bk // LANES), axis=1))
            accl = (jnp.concatenate([alpha, alpha], axis=1) * accl
                    + jnp.dot(p.astype(BF16), vblk, preferred_element_type=F32))
            out[r] = (m_new, accl)
        return tuple(out)

    init = (jnp.full((rows, LANES), -jnp.inf, F32), jnp.zeros((rows, 2 * HEAD_DIM), F32))
    per_q = bq // bk
    carries = lax.fori_loop(0, qi * per_q, lambda kj, cr: update(kj, cr), (init,) * groups)
    for d in range(per_q):
        carries = update(qi * per_q + d, carries, diag_offset=d * bk)
    for r, (_, accl) in enumerate(carries):
        rs = slice(r * rows, (r + 1) * rows)
        o_ref[rs, :] = (accl[:, :HEAD_DIM] / accl[:, HEAD_DIM:] * _silu(ga_ref[rs, :])).astype(BF16)


def _fox_prompt(z3, k_all, v_all, layer, c_rows):
    bsz, length, _ = z3.shape
    bq = min(FOX_BQ, length)
    bk = min(FOX_BK, bq)
    rows = min(FOX_ROWS, bq)
    assert length % bq == 0 and bq % bk == 0 and bq % rows == 0
    nk = length // bk
    c5 = c_rows.reshape(bsz, HEADS, nk, 1, bk)
    hpw = WIDTH // HEAD_DIM
    return pl.pallas_call(
        functools.partial(_fox_kernel, bq=bq, bk=bk, rows=rows, scale=HEAD_DIM ** -0.5),
        grid=(bsz, HEADS, length // bq),
        in_specs=[
            pl.BlockSpec((None, bq, HEAD_DIM), lambda b, h, i: (b, i, Z_Q * hpw + h)),
            pl.BlockSpec((None, None, length, HEAD_DIM), lambda b, h, i: (layer, b, 0, h)),
            pl.BlockSpec((None, None, length, HEAD_DIM), lambda b, h, i: (layer, b, 0, h)),
            pl.BlockSpec((None, None, nk, 1, bk), lambda b, h, i: (b, h, 0, 0, 0)),
            pl.BlockSpec((None, bq, HEAD_DIM), lambda b, h, i: (b, i, Z_GATE_A * hpw + h)),
        ],
        out_specs=pl.BlockSpec((None, bq, HEAD_DIM), lambda b, h, i: (b, i, h)),
        out_shape=jax.ShapeDtypeStruct((bsz, length, WIDTH), BF16),
        scratch_shapes=[pltpu.VMEM((length, HEAD_DIM), BF16), pltpu.VMEM((length, 2 * HEAD_DIM), BF16)],
        compiler_params=_params("arbitrary", "arbitrary", "arbitrary"),
        name="fox_prompt",
    )(z3, k_all, v_all, c5, z3)


def _fox_decode_kernel(q_ref, kn_ref, vn_ref, kp_ref, vp_ref, cp_ref, cn_ref, ga_ref, o_ref, *, scale):
    n = q_ref.shape[0]
    past = kp_ref.shape[0]
    q = (q_ref[...] * scale).astype(BF16)
    cp = cp_ref[...]
    t_past = _mm_nt(q, kp_ref[...]) - (cp - cp[:, past - 1:past])
    row = lax.broadcasted_iota(jnp.int32, (n, n), 0)
    col = lax.broadcasted_iota(jnp.int32, (n, n), 1)
    t_new = jnp.where(row >= col, _mm_nt(q, kn_ref[...]) - cn_ref[...], -jnp.inf)
    m = jnp.maximum(jnp.max(t_past, axis=1, keepdims=True), jnp.max(t_new, axis=1, keepdims=True))
    p_past = jnp.exp(t_past - m)
    p_new = jnp.exp(t_new - m)
    l = jnp.sum(p_past, axis=1, keepdims=True) + jnp.sum(p_new, axis=1, keepdims=True)
    acc = _mm(p_past, vp_ref[...]) + _mm(p_new, vn_ref[...])
    o_ref[...] = (acc / l * _silu(ga_ref[...])).astype(BF16)


def _fox_decode(z3, k_all, v_all, n_tok, cache_k, cache_v, layer, cp_rows, cn_rows):
    bsz = z3.shape[0]
    past = cache_k.shape[2]
    hpw = WIDTH // HEAD_DIM
    cp5 = cp_rows.reshape(-1, bsz, HEADS, 1, past)
    cn4 = cn_rows.reshape(bsz, HEADS, 1, n_tok)
    zblk = lambda seg: pl.BlockSpec((None, n_tok, HEAD_DIM), lambda b, h: (b, 0, seg * hpw + h))
    cache_k = cache_k.reshape(cache_k.shape[:3] + (WIDTH,))
    cache_v = cache_v.reshape(cache_v.shape[:3] + (WIDTH,))
    cblk = pl.BlockSpec((None, None, past, HEAD_DIM), lambda b, h: (layer, b, 0, h))
    nblk = pl.BlockSpec((None, None, n_tok, HEAD_DIM), lambda b, h: (layer, b, 0, h))
    return pl.pallas_call(
        functools.partial(_fox_decode_kernel, scale=HEAD_DIM ** -0.5),
        grid=(bsz, HEADS),
        in_specs=[
            zblk(Z_Q), nblk, nblk, cblk, cblk,
            pl.BlockSpec((None, None, None, 1, past), lambda b, h: (layer, b, h, 0, 0)),
            pl.BlockSpec((None, None, 1, n_tok), lambda b, h: (b, h, 0, 0)),
            zblk(Z_GATE_A),
        ],
        out_specs=pl.BlockSpec((None, n_tok, HEAD_DIM), lambda b, h: (b, 0, h)),
        out_shape=jax.ShapeDtypeStruct((bsz, n_tok, WIDTH), BF16),
        compiler_params=_params("arbitrary", "arbitrary"),
        name="fox_decode",
    )(z3, k_all, v_all, cache_k, cache_v, cp5, cn4, z3)


def _gdn_kernel(qp_ref, kp_ref, vp_ref, cw_ref, c0_ref, cols_ref, gr_ref, gb_ref, nw_ref, s0_ref,
                og_ref, s_ref, xp_ref, *, chunk, scale):
    pad = SUBLANES

    @pl.when(pl.program_id(1) == 0)
    def _():
        s_ref[...] = s0_ref[...]
        xp_ref[0:pad, :] = c0_ref[...]

    xp_ref[pad:pad + chunk, 0:WIDTH] = qp_ref[...]
    xp_ref[pad:pad + chunk, WIDTH:2 * WIDTH] = kp_ref[...]
    xp_ref[pad:pad + chunk, 2 * WIDTH:3 * WIDTH] = vp_ref[...]
    xp = xp_ref[...]
    acc = xp * cw_ref[0:1, :]
    for i in range(1, CONV_W):
        acc = pltpu.roll(acc, 1, axis=0) + xp * cw_ref[i:i + 1, :]
    conv = _silu(acc[pad:pad + chunk])
    xp_ref[0:pad, :] = xp_ref[chunk:chunk + pad, :]

    cols = cols_ref[...]
    row = lax.broadcasted_iota(jnp.int32, (chunk, chunk), 0)
    col = lax.broadcasted_iota(jnp.int32, (chunk, chunk), 1)
    incl = row >= col
    strict = row > col
    sizes = [INV_BASE << i for i in range(int(math.log2(chunk // INV_BASE)) + 1)]
    same = [(row >> int(math.log2(b))) == (col >> int(math.log2(b))) for b in sizes[:-1]]
    diag_mask = strict & same[0]
    off_masks = [strict & ~same[i] & (same[i + 1] if i + 1 < len(same) else True)
                 for i in range(len(same))]

    heads = range(HEADS)
    sl = lambda seg, h: slice(seg * WIDTH + h * HEAD_DIM, seg * WIDTH + (h + 1) * HEAD_DIM)
    q, k, v, gc, beta, e, lmat, qk = ([None] * HEADS for _ in range(8))
    for h in heads:
        qh, kh = conv[:, sl(0, h)], conv[:, sl(1, h)]
        q[h] = qh * (lax.rsqrt(jnp.sum(qh * qh, axis=1, keepdims=True) + RMS_EPS) * scale)
        k[h] = kh * lax.rsqrt(jnp.sum(kh * kh, axis=1, keepdims=True) + RMS_EPS)
        v[h] = conv[:, sl(2, h)]
        gc[h] = cols[:, h:h + 1]
        beta[h] = cols[:, HEADS + h:HEADS + h + 1]
        e[h] = jnp.exp(gc[h])
    for h in heads:
        decay = jnp.exp(jnp.where(incl, gc[h] - gr_ref[h:h + 1, :], -jnp.inf))
        kq = _mm_nt(jnp.concatenate([k[h] * beta[h], q[h]], axis=0), k[h])
        lmat[h] = jnp.where(strict, kq[:chunk] * decay, 0.0)
        qk[h] = jnp.where(incl, kq[chunk:] * decay, 0.0)
    xpow = [jnp.where(diag_mask, lmat[h], 0.0) for h in heads]
    nm = [-xpow[h] for h in heads]
    for _ in range(int(math.log2(INV_BASE)) - 1):
        xpow = [_mm(xpow[h], xpow[h]) for h in heads]
        nm = [nm[h] + xpow[h] + _mm(nm[h], xpow[h]) for h in heads]
    for off in off_masks:
        loff = [jnp.where(off, lmat[h], 0.0) for h in heads]
        t_loff = [loff[h] + _mm(nm[h], loff[h]) for h in heads]
        nm = [nm[h] - (t_loff[h] + _mm(t_loff[h], nm[h])) for h in heads]
    uw = []
    for h in heads:
        rhs = jnp.concatenate([v[h] * beta[h], k[h] * (beta[h] * e[h])], axis=1)
        uw.append(rhs + _mm(nm[h], rhs))
    s_old = [s_ref[h] for h in heads]
    ws_qs = [_mm(jnp.concatenate([uw[h][:, HEAD_DIM:], q[h] * e[h]], axis=0), s_old[h]) for h in heads]
    v_new = [uw[h][:, :HEAD_DIM] - ws_qs[h][:chunk] for h in heads]
    for h in heads:
        g_last = gc[h][chunk - 1:chunk, :]
        k_tail = k[h] * jnp.exp(g_last - gc[h])
        s_ref[h] = s_old[h] * jnp.exp(g_last) + _mm_tn(k_tail, v_new[h])
    for h in heads:
        o = ws_qs[h][chunk:] + _mm(qk[h], v_new[h])
        o = o * lax.rsqrt(jnp.mean(o * o, axis=1, keepdims=True) + RMS_EPS) * nw_ref[...]
        og_ref[:, sl(0, h)] = (o * _silu(gb_ref[:, sl(0, h)])).astype(BF16)


def _gdn(z3, conv_w, conv0, cols, gc_rows, norm_w, s0):
    bsz, length, _ = z3.shape
    chunk = GDN_CHUNK
    assert length % chunk == 0
    nc = length // chunk
    gr = gc_rows.reshape(bsz, HEADS, nc, chunk).transpose(0, 2, 1, 3)
    c0 = jnp.pad(conv0, ((0, 0), (SUBLANES - (CONV_W - 1), 0), (0, 0)))
    zblk = lambda seg: pl.BlockSpec((None, chunk, WIDTH), lambda b, l: (b, l, seg))
    return pl.pallas_call(
        functools.partial(_gdn_kernel, chunk=chunk, scale=HEAD_DIM ** -0.5),
        grid=(bsz, nc),
        in_specs=[
            zblk(Z_QKV_B), zblk(Z_QKV_B + 1), zblk(Z_QKV_B + 2),
            pl.BlockSpec((CONV_W, CONV_CH), lambda b, l: (0, 0)),
            pl.BlockSpec((None, SUBLANES, CONV_CH), lambda b, l: (b, 0, 0)),
            pl.BlockSpec((None, chunk, LANES), lambda b, l: (b, l, 0)),
            pl.BlockSpec((None, None, HEADS, chunk), lambda b, l: (b, l, 0, 0)),
            zblk(Z_GATE_B),
            pl.BlockSpec((1, HEAD_DIM), lambda b, l: (0, 0)),
            pl.BlockSpec((None, HEADS, HEAD_DIM, HEAD_DIM), lambda b, l: (b, 0, 0, 0)),
        ],
        out_specs=[
            pl.BlockSpec((None, chunk, WIDTH), lambda b, l: (b, l, 0)),
            pl.BlockSpec((None, HEADS, HEAD_DIM, HEAD_DIM), lambda b, l: (b, 0, 0, 0)),
        ],
        out_shape=[
            jax.ShapeDtypeStruct((bsz, length, WIDTH), BF16),
            jax.ShapeDtypeStruct((bsz, HEADS, HEAD_DIM, HEAD_DIM), F32),
        ],
        scratch_shapes=[pltpu.VMEM((chunk + SUBLANES, CONV_CH), F32)],
        compiler_params=_params("arbitrary", "arbitrary"),
        name="gdn",
    )(z3, z3, z3, conv_w, c0, cols, gr, z3, norm_w.reshape(1, HEAD_DIM), s0)


def _out_kernel(oa_ref, ob_ref, ma_ref, mb_ref, x_ref, p_ref, wfox_ref, wgdn_ref, wout_ref,
                wplp_ref, wplg_ref, vec_ref, o_ref, *, alpha):
    ya = jnp.dot(oa_ref[...], wfox_ref[...], preferred_element_type=F32)
    yb = jnp.dot(ob_ref[...], wgdn_ref[...], preferred_element_type=F32)
    merged = _sigmoid(ma_ref[...]) * ya + _sigmoid(mb_ref[...]) * yb
    y = alpha * x_ref[...] + _mm(merged, wout_ref[...])
    yc = y - jnp.mean(y, axis=1, keepdims=True)
    var = jnp.mean(yc * yc, axis=1, keepdims=True)
    x1 = yc * lax.rsqrt(var + LN_EPS) * vec_ref[0:1, :] + vec_ref[1:2, :]
    e = _mm(p_ref[...], wplp_ref[...])
    e = e * lax.rsqrt(jnp.mean(e * e, axis=1, keepdims=True) + RMS_EPS) * vec_ref[2:3, :]
    o_ref[...] = x1 + _sigmoid(_mm(x1, wplg_ref[...])) * e


def _out_block(oa, ob, z2d, x2d, p_all, layer, wfox, wgdn, wout, wplp, wplg, vec, alpha):
    t_rows, d = x2d.shape
    ple = p_all.shape[-1]
    tm = min(256, t_rows)
    dpw = d // WIDTH
    assert d % WIDTH == 0 and Z_M_A % dpw == 0 and Z_M_B % dpw == 0
    const = lambda shape: pl.BlockSpec(shape, lambda r: (0,) * len(shape), pipeline_mode=pl.Buffered(1))
    return pl.pallas_call(
        functools.partial(_out_kernel, alpha=alpha),
        grid=(t_rows // tm,),
        in_specs=[
            pl.BlockSpec((tm, WIDTH), lambda r: (r, 0)),
            pl.BlockSpec((tm, WIDTH), lambda r: (r, 0)),
            pl.BlockSpec((tm, d), lambda r: (r, Z_M_A // dpw)),
            pl.BlockSpec((tm, d), lambda r: (r, Z_M_B // dpw)),
            pl.BlockSpec((tm, d), lambda r: (r, 0)),
            pl.BlockSpec((None, tm, ple), lambda r: (layer, r, 0)),
            const((WIDTH, d)), const((WIDTH, d)), const((d, d)), const((ple, d)), const((d, d)),
            const((SUBLANES, d)),
        ],
        out_specs=pl.BlockSpec((tm, d), lambda r: (r, 0)),
        out_shape=jax.ShapeDtypeStruct((t_rows, d), F32),
        compiler_params=_params("arbitrary"),
        name="out_block",
    )(oa, ob, z2d, z2d, x2d, p_all, wfox, wgdn, wout, wplp, wplg, vec)


def _layer(x2d, bsz, n_tok, p_all, layer, wts, conv0, s0, alpha, k_all, v_all, fox_cache=None):
    lp = x2d.shape[0] // bsz
    z2d, small, k_all, v_all = _proj(x2d, wts["w_big"], wts["w_small"], wts["par"], k_all, v_all, layer)
    z3 = z2d.reshape(bsz, lp, -1)
    k4 = k_all.reshape(-1, bsz, lp, WIDTH)
    v4 = v_all.reshape(-1, bsz, lp, WIDTH)
    small3 = small.reshape(bsz, lp, LANES)
    logf_all = small3[:, :, 0:HEADS]
    logf = logf_all[:, :n_tok]
    valid = (jnp.arange(lp) < n_tok)[None, :, None]
    g = jnp.where(valid, small3[:, :, HEADS:2 * HEADS], 0.0)
    beta = jnp.where(valid, small3[:, :, 2 * HEADS:3 * HEADS], 0.0)

    to_rows = lambda a: a.transpose(0, 2, 1).reshape(bsz * HEADS, a.shape[1])
    c_rows = _cumsum_rows(to_rows(logf_all))
    gc_rows = _cumsum_rows(to_rows(g), period=GDN_CHUNK)
    gc = gc_rows.reshape(bsz, HEADS, lp).transpose(0, 2, 1)
    cols = jnp.concatenate([gc, beta, jnp.zeros((bsz, lp, LANES - 2 * HEADS), F32)], axis=-1)

    if fox_cache is None:
        oa = _fox_prompt(z3, k4, v4, layer, c_rows)
    else:
        cache_k, cache_v, cp_rows = fox_cache
        oa = _fox_decode(z3, k4, v4, n_tok, cache_k, cache_v, layer, cp_rows, c_rows[:, :n_tok])
        oa = jnp.pad(oa, ((0, 0), (0, lp - n_tok), (0, 0)))
    ob, s_new = _gdn(z3, wts["conv_w"], conv0, cols, gc_rows, wts["norm_w"], s0)

    x_out = _out_block(oa.reshape(bsz * lp, WIDTH), ob.reshape(bsz * lp, WIDTH), z2d, x2d, p_all, layer,
                       wts["wfox"], wts["wgdn"], wts["wout"], wts["wplp"], wts["wplg"], wts["vec"], alpha)

    new_conv = z3[:, n_tok - (CONV_W - 1):n_tok, Z_QKV_B * WIDTH:Z_QKV_B * WIDTH + CONV_CH]
    return x_out, k_all, v_all, logf, new_conv, s_new


def kernel(x_prompt, x_sample, cache_fox_k, cache_fox_v, cache_fox_logf, state_gdn_conv, state_gdn, p_prompt, p_sample, w_in, fox_f_bias, gdn_conv_w, gdn_a_log, gdn_dt_bias, gdn_norm_w, w_out_fox, w_out_gdn, w_out, ln_g, ln_b, w_pl_proj, w_pl_gate, pl_norm_w):
    depth = w_in.shape[0]
    bp, seq, d = x_prompt.shape
    bs, dseq, _ = x_sample.shape
    past = cache_fox_k.shape[2]
    ple = p_prompt.shape[-1]
    assert d % WIDTH == 0 and dseq >= CONV_W - 1 and seq >= CONV_W - 1
    alpha = (2 * depth) ** 0.25

    o_f = 3 * WIDTH
    o_ga = o_f + HEADS
    o_qkv = o_ga + WIDTH
    o_a = o_qkv + CONV_CH
    o_gb = o_a + 2 * HEADS
    w_big = jnp.concatenate([w_in[..., :WIDTH], w_in[..., o_ga:o_a], w_in[..., o_gb:],
                             w_in[..., WIDTH:o_f]], axis=-1).astype(BF16)
    w_small = jnp.concatenate([w_in[..., o_f:o_ga], w_in[..., o_a:o_gb],
                               jnp.zeros((depth, d, LANES - 3 * HEADS), F32)], axis=-1)
    w_small_hi = w_small.astype(BF16)
    w_small = jnp.stack([w_small_hi, (w_small - w_small_hi.astype(F32)).astype(BF16)], axis=1)
    zpad = jnp.zeros((depth, LANES - 2 * HEADS), F32)
    par = jnp.stack([jnp.concatenate([fox_f_bias, gdn_dt_bias, zpad], axis=-1),
                     jnp.concatenate([jnp.zeros_like(fox_f_bias), gdn_a_log, zpad], axis=-1)], axis=1)
    par = jnp.pad(par, ((0, 0), (0, SUBLANES - 2), (0, 0)))
    vec = jnp.pad(jnp.stack([ln_g, ln_b, pl_norm_w], axis=1), ((0, 0), (0, SUBLANES - 3), (0, 0)))
    wfox, wgdn, wout = w_out_fox.astype(BF16), w_out_gdn.astype(BF16), w_out.astype(BF16)
    wplp, wplg = w_pl_proj.astype(BF16), w_pl_gate.astype(BF16)

    def layer_weights(i):
        return dict(w_big=w_big[i], w_small=w_small[i], par=par[i], conv_w=gdn_conv_w[i],
                    norm_w=gdn_norm_w[i], wfox=wfox[i], wgdn=wgdn[i], wout=wout[i],
                    wplp=wplp[i], wplg=wplg[i], vec=vec[i])

    lp_s = -(-dseq // GDN_CHUNK) * GDN_CHUNK
    xs = jnp.pad(x_sample, ((0, 0), (0, lp_s - dseq), (0, 0))).reshape(bs * lp_s, d)
    ps_all = jnp.pad(p_sample, ((0, 0), (0, 0), (0, lp_s - dseq), (0, 0))).reshape(depth, bs * lp_s, ple)
    xp = x_prompt.reshape(bp * seq, d)
    pp_all = p_prompt.reshape(depth, bp * seq, ple)

    cp_rows = _cumsum_rows(cache_fox_logf.transpose(0, 1, 3, 2).reshape(depth * bs * HEADS, past))

    conv0_p = jnp.zeros((bp, CONV_W - 1, CONV_CH), F32)
    s0_p = jnp.zeros((bp, HEADS, HEAD_DIM, HEAD_DIM), F32)
    kp = jnp.zeros((depth, bp * seq, WIDTH), F32)
    vp = jnp.zeros((depth, bp * seq, WIDTH), F32)
    ks = jnp.zeros((depth, bs * lp_s, WIDTH), F32)
    vs = jnp.zeros((depth, bs * lp_s, WIDTH), F32)
    outs_p, outs_s = [], []
    for i in range(depth):
        wts = layer_weights(i)
        xp, kp, vp, *rest_p = _layer(xp, bp, seq, pp_all, i, wts, conv0_p, s0_p, alpha, kp, vp)
        xs, ks, vs, *rest_s = _layer(xs, bs, dseq, ps_all, i, wts, state_gdn_conv[i], state_gdn[i], alpha,
                                     ks, vs, fox_cache=(cache_fox_k, cache_fox_v, cp_rows))
        outs_p.append(rest_p)
        outs_s.append(rest_s)
    stack = lambda outs, j: jnp.stack([o[j] for o in outs])
    heads_p = lambda a: a.reshape(depth, bp, seq, HEADS, HEAD_DIM)
    heads_s = lambda a: a.reshape(depth, bs, lp_s, HEADS, HEAD_DIM)[:, :, :dseq]
    y_prompt = xp.reshape(bp, seq, d)
    y_sample = xs.reshape(bs, lp_s, d)[:, :dseq]
    return (y_prompt, y_sample,
            heads_p(kp), heads_p(vp), stack(outs_p, 0), stack(outs_p, 1), stack(outs_p, 2),
            heads_s(ks), heads_s(vs), stack(outs_s, 0), stack(outs_s, 1), stack(outs_s, 2))
```

```python
import functools
import math

import jax
import jax.numpy as jnp
from jax import lax
from jax.experimental import pallas as pl
from jax.experimental.pallas import tpu as pltpu

F32 = jnp.float32
BF16 = jnp.bfloat16

HEADS = 8
HEAD_DIM = 128
WIDTH = HEADS * HEAD_DIM
CONV_W = 4
CONV_CH = 3 * WIDTH
LN_EPS = 1e-5
RMS_EPS = 1e-6
LOG2E = math.log2(math.e)
LANES = 128
SUBLANES = 8
GDN_CHUNK = 128
INV_BASE = 8
FOX_BQ = 1024
FOX_BK = 1024
FOX_ROWS = 256
FOX_STRIP = 16
VMEM_LIMIT = 56 * 1024 * 1024

Z_Q, Z_GATE_A, Z_QKV_B, Z_GATE_B, Z_M_A, Z_M_B, Z_END = 0, 1, 2, 5, 6, 8, 10


def _params(*sem):
    return pltpu.CompilerParams(dimension_semantics=sem, vmem_limit_bytes=VMEM_LIMIT)


def _mm(a, b):
    return jnp.dot(a.astype(BF16), b.astype(BF16), preferred_element_type=F32)


def _mm_nt(a, b):
    return lax.dot_general(a.astype(BF16), b.astype(BF16), (((1,), (1,)), ((), ())),
                           preferred_element_type=F32)


def _mm_tn(a, b):
    return lax.dot_general(a.astype(BF16), b.astype(BF16), (((0,), (0,)), ((), ())),
                           preferred_element_type=F32)


def _sigmoid(x):
    return 1.0 / (1.0 + jnp.exp(-x))


def _silu(x):
    return x * _sigmoid(x)


def _proj_kernel(x_ref, w_ref, ws_ref, par_ref, k_in_ref, v_in_ref, z_ref, s_ref, k_ref, v_ref, xb_ref,
                 *, z_tiles, kv_tiles):
    del k_in_ref, v_in_ref
    j = pl.program_id(1)

    @pl.when(j == 0)
    def _():
        x = x_ref[...]
        xb = x.astype(BF16)
        xb_ref[...] = xb
        x_lo = (x - xb.astype(F32)).astype(BF16)
        raw = (jnp.dot(xb, ws_ref[0], preferred_element_type=F32)
               + jnp.dot(x_lo, ws_ref[0], preferred_element_type=F32)
               + jnp.dot(xb, ws_ref[1], preferred_element_type=F32))
        y = raw + par_ref[0:1, :]
        neg_a = -jnp.exp(par_ref[1:2, :])
        t = jnp.log1p(jnp.exp(-jnp.abs(y)))
        logf = jnp.minimum(y, 0.0) - t
        g = neg_a * (jnp.maximum(y, 0.0) + t)
        beta = _sigmoid(raw)
        lane = lax.broadcasted_iota(jnp.int32, raw.shape, 1)
        s_ref[...] = jnp.where(lane < HEADS, logf,
                               jnp.where(lane < 2 * HEADS, g,
                                         jnp.where(lane < 3 * HEADS, beta, 0.0)))

    @pl.when(j < z_tiles)
    def _():
        z_ref[...] = jnp.dot(xb_ref[...], w_ref[...], preferred_element_type=F32)

    @pl.when((j >= z_tiles) & (j < z_tiles + kv_tiles))
    def _():
        k_ref[...] = jnp.dot(xb_ref[...], w_ref[...], preferred_element_type=F32)

    @pl.when(j >= z_tiles + kv_tiles)
    def _():
        v_ref[...] = jnp.dot(xb_ref[...], w_ref[...], preferred_element_type=F32)


def _proj(x2d, w_big, w_small, par, k_all, v_all, layer):
    t_rows, d = x2d.shape
    tm = min(1024, t_rows)
    tn = 512
    kv_tiles = WIDTH // tn
    z_tiles = w_big.shape[-1] // tn - 2 * kv_tiles
    clamp = lambda j, lo: jnp.clip(j - lo, 0, kv_tiles - 1)
    return pl.pallas_call(
        functools.partial(_proj_kernel, z_tiles=z_tiles, kv_tiles=kv_tiles),
        grid=(t_rows // tm, z_tiles + 2 * kv_tiles),
        in_specs=[
            pl.BlockSpec((tm, d), lambda i, j: (i, 0)),
            pl.BlockSpec((None, d, tn), lambda i, j: (layer, 0, j)),
            pl.BlockSpec((None, 2, d, LANES), lambda i, j: (layer, 0, 0, 0)),
            pl.BlockSpec((None, SUBLANES, LANES), lambda i, j: (layer, 0, 0)),
            pl.BlockSpec(memory_space=pl.ANY),
            pl.BlockSpec(memory_space=pl.ANY),
        ],
        out_specs=[
            pl.BlockSpec((tm, tn), lambda i, j: (i, jnp.minimum(j, z_tiles - 1))),
            pl.BlockSpec((tm, LANES), lambda i, j: (i, 0)),
            pl.BlockSpec((None, tm, tn), lambda i, j: (layer, i, clamp(j, z_tiles))),
            pl.BlockSpec((None, tm, tn), lambda i, j: (layer, i, clamp(j, z_tiles + kv_tiles))),
        ],
        out_shape=[
            jax.ShapeDtypeStruct((t_rows, z_tiles * tn), F32),
            jax.ShapeDtypeStruct((t_rows, LANES), F32),
            jax.ShapeDtypeStruct(k_all.shape, F32),
            jax.ShapeDtypeStruct(v_all.shape, F32),
        ],
        input_output_aliases={4: 2, 5: 3},
        scratch_shapes=[pltpu.VMEM((tm, d), BF16)],
        compiler_params=_params("arbitrary", "arbitrary"),
        name="proj",
    )(x2d, w_big, w_small, par, k_all, v_all)


def _cumsum_kernel(x_ref, init_ref, o_ref, carry_ref, *, blk, period):
    r = lax.broadcasted_iota(jnp.int32, (blk, blk), 0)
    c = lax.broadcasted_iota(jnp.int32, (blk, blk), 1)
    keep = r <= c
    if period is not None:
        keep = keep & ((r // period) == (c // period))
    y = jnp.dot(x_ref[...], keep.astype(F32), precision=lax.Precision.HIGHEST,
                preferred_element_type=F32)
    if period is None:
        @pl.when(pl.program_id(0) == 0)
        def _():
            carry_ref[...] = init_ref[...]

        y = y + carry_ref[...]
        carry_ref[...] = y[:, blk - 1:blk]
    o_ref[...] = y


def _cumsum_rows(x, period=None, init=None):
    rows, length = x.shape
    blk = min(512, length)
    assert length % blk == 0 and (period is None or (blk % period == 0 and init is None))
    if init is None:
        init = jnp.zeros((rows, 1), F32)
    return pl.pallas_call(
        functools.partial(_cumsum_kernel, blk=blk, period=period),
        grid=(length // blk,),
        in_specs=[pl.BlockSpec((rows, blk), lambda j: (0, j)), pl.BlockSpec((rows, 1), lambda j: (0, 0))],
        out_specs=pl.BlockSpec((rows, blk), lambda j: (0, j)),
        out_shape=jax.ShapeDtypeStruct((rows, length), F32),
        scratch_shapes=[pltpu.VMEM((rows, 1), F32)],
        compiler_params=_params("arbitrary"),
        name="cumsum",
    )(x, init)


def _fox_kernel(q_ref, k_ref, v_ref, c_ref, ga_ref, o_ref, kb_ref, vb_ref, *, bq, bk, rows, scale):
    qi = pl.program_id(2)
    length = k_ref.shape[0]
    groups = bq // rows

    @pl.when(qi == 0)
    def _():
        kb_ref[...] = k_ref[...].astype(BF16)
        vb_ref[:, :HEAD_DIM] = v_ref[...].astype(BF16)
        vb_ref[:, HEAD_DIM:] = jnp.ones((length, LANES), BF16)

    q = (q_ref[...] * (scale * LOG2E)).astype(BF16)
    qs = [q[r * rows:(r + 1) * rows] for r in range(groups)]
    row = lax.broadcasted_iota(jnp.int32, (FOX_STRIP, bk), 0)
    col = lax.broadcasted_iota(jnp.int32, (FOX_STRIP, bk), 1)

    def update(kj, carries, diag_offset=None):
        start = pl.multiple_of(kj * bk, bk)
        kblk = kb_ref[pl.ds(start, bk), :]
        vblk = vb_ref[pl.ds(start, bk), :]
        ck = c_ref[kj] * LOG2E
        ss = {}
        for r in range(groups):
            if diag_offset is not None and diag_offset > r * rows + rows - 1:
                continue
            ss[r] = lax.dot_general(qs[r], kblk, (((1,), (1,)), ((), ())), preferred_element_type=F32)
        out = list(carries)
        lane_blocks = [slice(c * LANES, (c + 1) * LANES) for c in range(bk // LANES)]
        for r, s in ss.items():
            m, acc, l = carries[r]
            masked = diag_offset is not None and diag_offset + bk - 1 > r * rows
            m_parts, a_parts, p_parts = [], [], []
            for i in range(rows // FOX_STRIP):
                sl = slice(i * FOX_STRIP, (i + 1) * FOX_STRIP)
                t = s[sl] - ck
                if masked:
                    t = jnp.where(col + (diag_offset - r * rows - i * FOX_STRIP) <= row, t, -jnp.inf)
                mx = t[:, lane_blocks[0]]
                for c in lane_blocks[1:]:
                    mx = jnp.maximum(mx, t[:, c])
                m_new = jnp.maximum(m[sl], jnp.broadcast_to(jnp.max(mx, axis=1, keepdims=True), mx.shape))
                m_parts.append(m_new)
                a_parts.append(jnp.exp2(m[sl] - m_new))
                p_parts.append(jnp.concatenate([jnp.exp2(t[:, c] - m_new).astype(BF16) for c in lane_blocks],
                                               axis=1))
            alpha = jnp.concatenate(a_parts, axis=0)
            pv = jnp.dot(jnp.concatenate(p_parts, axis=0), vblk, preferred_element_type=F32)
            out[r] = (jnp.concatenate(m_parts, axis=0), alpha * acc + pv[:, :HEAD_DIM],
                      alpha * l + pv[:, HEAD_DIM:])
        return tuple(out)

    init = (jnp.full((rows, LANES), -jnp.inf, F32), jnp.zeros((rows, HEAD_DIM), F32),
            jnp.zeros((rows, LANES), F32))
    per_q = bq // bk
    def past_q_block(j, cr):
        for d in range(per_q):
            cr = update(j * per_q + d, cr)
        return cr

    carries = lax.fori_loop(0, qi, past_q_block, (init,) * groups)
    for d in range(per_q):
        carries = update(qi * per_q + d, carries, diag_offset=d * bk)
    for r, (_, acc, l) in enumerate(carries):
        rs = slice(r * rows, (r + 1) * rows)
        o_ref[rs, :] = (acc / l * _silu(ga_ref[rs, :])).astype(BF16)


def _fox_prompt(z3, k_all, v_all, layer, c_rows):
    bsz, length, _ = z3.shape
    bq = min(FOX_BQ, length)
    bk = min(FOX_BK, bq)
    rows = min(FOX_ROWS, bq)
    assert length % bq == 0 and bq % bk == 0 and bq % rows == 0
    nk = length // bk
    c5 = c_rows.reshape(bsz, HEADS, nk, 1, bk)
    hpw = WIDTH // HEAD_DIM
    return pl.pallas_call(
        functools.partial(_fox_kernel, bq=bq, bk=bk, rows=rows, scale=HEAD_DIM ** -0.5),
        grid=(bsz, HEADS, length // bq),
        in_specs=[
            pl.BlockSpec((None, bq, HEAD_DIM), lambda b, h, i: (b, i, Z_Q * hpw + h)),
            pl.BlockSpec((None, None, length, HEAD_DIM), lambda b, h, i: (layer, b, 0, h)),
            pl.BlockSpec((None, None, length, HEAD_DIM), lambda b, h, i: (layer, b, 0, h)),
            pl.BlockSpec((None, None, nk, 1, bk), lambda b, h, i: (b, h, 0, 0, 0)),
            pl.BlockSpec((None, bq, HEAD_DIM), lambda b, h, i: (b, i, Z_GATE_A * hpw + h)),
        ],
        out_specs=pl.BlockSpec((None, bq, HEAD_DIM), lambda b, h, i: (b, i, h)),
        out_shape=jax.ShapeDtypeStruct((bsz, length, WIDTH), BF16),
        scratch_shapes=[pltpu.VMEM((length, HEAD_DIM), BF16), pltpu.VMEM((length, 2 * HEAD_DIM), BF16)],
        compiler_params=_params("arbitrary", "arbitrary", "arbitrary"),
        name="fox_prompt",
    )(z3, k_all, v_all, c5, z3)


def _fox_decode_kernel(q_ref, kn_ref, vn_ref, cn_ref, ga_ref, kp_ref, vp_ref, cp_ref, o_ref,
                       m_ref, acc_ref, l_ref, *, n, scale):
    j = pl.program_id(1)
    nrow = q_ref.shape[0]
    q = (q_ref[...] * (scale * LOG2E)).astype(BF16)

    @pl.when(j == 0)
    def _():
        m_ref[...] = jnp.full(m_ref.shape, -jnp.inf, F32)
        acc_ref[...] = jnp.zeros(acc_ref.shape, F32)
        l_ref[...] = jnp.zeros(l_ref.shape, F32)

    def update(kblk, vblk, c_row, visible):
        t = jnp.where(visible, _mm_nt(q, kblk) - c_row * LOG2E, -jnp.inf)
        m = m_ref[...]
        m_new = jnp.maximum(m, jnp.broadcast_to(jnp.max(t, axis=1, keepdims=True), m.shape))
        alpha = jnp.exp2(m - m_new)
        p = jnp.concatenate([jnp.exp2(t[:, c * LANES:(c + 1) * LANES] - m_new)
                             for c in range(t.shape[1] // LANES)], axis=1)
        m_ref[...] = m_new
        l_ref[...] = alpha * l_ref[...] + jnp.broadcast_to(jnp.sum(p, axis=1, keepdims=True), m.shape)
        acc_ref[...] = alpha * acc_ref[...] + _mm(p, vblk)

    cols = kp_ref.shape[0]
    row_head = lax.broadcasted_iota(jnp.int32, (nrow, cols), 0) // n
    col_head = lax.broadcasted_iota(jnp.int32, (nrow, cols), 1) % HEADS
    update(kp_ref[...], vp_ref[...], cp_ref[...], row_head == col_head)

    @pl.when(j == pl.num_programs(1) - 1)
    def _():
        r = lax.broadcasted_iota(jnp.int32, (nrow, nrow), 0)
        c = lax.broadcasted_iota(jnp.int32, (nrow, nrow), 1)
        update(kn_ref[...], vn_ref[...], cn_ref[...], (r // n == c // n) & (c % n <= r % n))
        o_ref[...] = (acc_ref[...] / l_ref[...] * _silu(ga_ref[...])).astype(BF16)


def _fox_decode(z3, k4, v4, n_tok, cache_k, cache_v, layer, cp_flat, cn_rows):
    bsz = z3.shape[0]
    past = cache_k.shape[2]
    nrow = HEADS * n_tok
    pc = min(1024, past)
    assert past % pc == 0 and nrow % LANES == 0

    def head_rows(a):
        return a.reshape(bsz, n_tok, HEADS, HEAD_DIM).transpose(0, 2, 1, 3).reshape(bsz, nrow, HEAD_DIM)

    seg = lambda s: z3[:, :n_tok, s * WIDTH:(s + 1) * WIDTH]
    k2 = cache_k.reshape(cache_k.shape[:2] + (past * HEADS, HEAD_DIM))
    v2 = cache_v.reshape(cache_v.shape[:2] + (past * HEADS, HEAD_DIM))
    rows_blk = pl.BlockSpec((None, nrow, HEAD_DIM), lambda b, j: (b, 0, 0))
    cache_blk = pl.BlockSpec((None, None, pc * HEADS, HEAD_DIM), lambda b, j: (layer, b, j, 0))
    o_rows = pl.pallas_call(
        functools.partial(_fox_decode_kernel, n=n_tok, scale=HEAD_DIM ** -0.5),
        grid=(bsz, past // pc),
        in_specs=[
            rows_blk, rows_blk, rows_blk,
            pl.BlockSpec((None, 1, nrow), lambda b, j: (b, 0, 0)),
            rows_blk, cache_blk, cache_blk,
            pl.BlockSpec((None, None, 1, pc * HEADS), lambda b, j: (layer, b, 0, j)),
        ],
        out_specs=rows_blk,
        out_shape=jax.ShapeDtypeStruct((bsz, nrow, HEAD_DIM), BF16),
        scratch_shapes=[pltpu.VMEM((nrow, LANES), F32), pltpu.VMEM((nrow, HEAD_DIM), F32),
                        pltpu.VMEM((nrow, LANES), F32)],
        compiler_params=_params("arbitrary", "arbitrary"),
        name="fox_decode",
    )(head_rows(seg(Z_Q)), head_rows(k4[layer, :, :n_tok]), head_rows(v4[layer, :, :n_tok]),
      cn_rows.reshape(bsz, 1, nrow), head_rows(seg(Z_GATE_A)), k2, v2, cp_flat)
    return o_rows.reshape(bsz, HEADS, n_tok, HEAD_DIM).transpose(0, 2, 1, 3).reshape(bsz, n_tok, WIDTH)


def _gdn_kernel(qp_ref, kp_ref, vp_ref, cw_ref, c0_ref, cols_ref, gr_ref, gb_ref, nw_ref, s0_ref,
                og_ref, s_ref, xp_ref, *, chunk, scale):
    pad = SUBLANES

    @pl.when(pl.program_id(1) == 0)
    def _():
        s_ref[...] = s0_ref[...]
        xp_ref[0:pad, :] = c0_ref[...]

    xp_ref[pad:pad + chunk, 0:WIDTH] = qp_ref[...]
    xp_ref[pad:pad + chunk, WIDTH:2 * WIDTH] = kp_ref[...]
    xp_ref[pad:pad + chunk, 2 * WIDTH:3 * WIDTH] = vp_ref[...]
    xp = xp_ref[...]
    acc = xp * cw_ref[0:1, :]
    for i in range(1, CONV_W):
        acc = pltpu.roll(acc, 1, axis=0) + xp * cw_ref[i:i + 1, :]
    conv = _silu(acc[pad:pad + chunk])
    xp_ref[0:pad, :] = xp_ref[chunk:chunk + pad, :]

    cols = cols_ref[...]
    row = lax.broadcasted_iota(jnp.int32, (chunk, chunk), 0)
    col = lax.broadcasted_iota(jnp.int32, (chunk, chunk), 1)
    incl = row >= col
    strict = row > col
    sizes = [INV_BASE << i for i in range(int(math.log2(chunk // INV_BASE)) + 1)]
    same = [(row >> int(math.log2(b))) == (col >> int(math.log2(b))) for b in sizes[:-1]]
    diag_mask = strict & same[0]
    off_masks = [strict & ~same[i] & (same[i + 1] if i + 1 < len(same) else True)
                 for i in range(len(same))]

    heads = range(HEADS)
    sl = lambda seg, h: slice(seg * WIDTH + h * HEAD_DIM, seg * WIDTH + (h + 1) * HEAD_DIM)
    q, k, v, gc, beta, e, lmat, qk = ([None] * HEADS for _ in range(8))
    for h in heads:
        qh, kh = conv[:, sl(0, h)], conv[:, sl(1, h)]
        q[h] = qh * (lax.rsqrt(jnp.sum(qh * qh, axis=1, keepdims=True) + RMS_EPS) * scale)
        k[h] = kh * lax.rsqrt(jnp.sum(kh * kh, axis=1, keepdims=True) + RMS_EPS)
        v[h] = conv[:, sl(2, h)]
        gc[h] = cols[:, h:h + 1]
        beta[h] = cols[:, HEADS + h:HEADS + h + 1]
        e[h] = jnp.exp(gc[h])
    for h in heads:
        decay = jnp.exp(jnp.where(incl, gc[h] - gr_ref[h:h + 1, :], -jnp.inf))
        kq = _mm_nt(jnp.concatenate([k[h] * beta[h], q[h]], axis=0), k[h])
        lmat[h] = jnp.where(strict, kq[:chunk] * decay, 0.0)
        qk[h] = jnp.where(incl, kq[chunk:] * decay, 0.0)
    xpow = [jnp.where(diag_mask, lmat[h], 0.0) for h in heads]
    nm = [-xpow[h] for h in heads]
    for _ in range(int(math.log2(INV_BASE)) - 1):
        xpow = [_mm(xpow[h], xpow[h]) for h in heads]
        nm = [nm[h] + xpow[h] + _mm(nm[h], xpow[h]) for h in heads]
    for off in off_masks:
        loff = [jnp.where(off, lmat[h], 0.0) for h in heads]
        t_loff = [loff[h] + _mm(nm[h], loff[h]) for h in heads]
        nm = [nm[h] - (t_loff[h] + _mm(t_loff[h], nm[h])) for h in heads]
    uw = []
    for h in heads:
        rhs = jnp.concatenate([v[h] * beta[h], k[h] * (beta[h] * e[h])], axis=1)
        uw.append(rhs + _mm(nm[h], rhs))
    s_old = [s_ref[h] for h in heads]
    ws_qs = [_mm(jnp.concatenate([uw[h][:, HEAD_DIM:], q[h] * e[h]], axis=0), s_old[h]) for h in heads]
    v_new = [uw[h][:, :HEAD_DIM] - ws_qs[h][:chunk] for h in heads]
    for h in heads:
        g_last = gc[h][chunk - 1:chunk, :]
        k_tail = k[h] * jnp.exp(g_last - gc[h])
        s_ref[h] = s_old[h] * jnp.exp(g_last) + _mm_tn(k_tail, v_new[h])
    for h in heads:
        o = ws_qs[h][chunk:] + _mm(qk[h], v_new[h])
        o = o * lax.rsqrt(jnp.mean(o * o, axis=1, keepdims=True) + RMS_EPS) * nw_ref[...]
        og_ref[:, sl(0, h)] = (o * _silu(gb_ref[:, sl(0, h)])).astype(BF16)


def _gdn(z3, conv_w, conv0, cols, gc_rows, norm_w, s0):
    bsz, length, _ = z3.shape
    chunk = GDN_CHUNK
    assert length % chunk == 0
    nc = length // chunk
    gr = gc_rows.reshape(bsz, HEADS, nc, chunk).transpose(0, 2, 1, 3)
    c0 = jnp.pad(conv0, ((0, 0), (SUBLANES - (CONV_W - 1), 0), (0, 0)))
    zblk = lambda seg: pl.BlockSpec((None, chunk, WIDTH), lambda b, l: (b, l, seg))
    return pl.pallas_call(
        functools.partial(_gdn_kernel, chunk=chunk, scale=HEAD_DIM ** -0.5),
        grid=(bsz, nc),
        in_specs=[
            zblk(Z_QKV_B), zblk(Z_QKV_B + 1), zblk(Z_QKV_B + 2),
            pl.BlockSpec((CONV_W, CONV_CH), lambda b, l: (0, 0)),
            pl.BlockSpec((None, SUBLANES, CONV_CH), lambda b, l: (b, 0, 0)),
            pl.BlockSpec((None, chunk, LANES), lambda b, l: (b, l, 0)),
            pl.BlockSpec((None, None, HEADS, chunk), lambda b, l: (b, l, 0, 0)),
            zblk(Z_GATE_B),
            pl.BlockSpec((1, HEAD_DIM), lambda b, l: (0, 0)),
            pl.BlockSpec((None, HEADS, HEAD_DIM, HEAD_DIM), lambda b, l: (b, 0, 0, 0)),
        ],
        out_specs=[
            pl.BlockSpec((None, chunk, WIDTH), lambda b, l: (b, l, 0)),
            pl.BlockSpec((None, HEADS, HEAD_DIM, HEAD_DIM), lambda b, l: (b, 0, 0, 0)),
        ],
        out_shape=[
            jax.ShapeDtypeStruct((bsz, length, WIDTH), BF16),
            jax.ShapeDtypeStruct((bsz, HEADS, HEAD_DIM, HEAD_DIM), F32),
        ],
        scratch_shapes=[pltpu.VMEM((chunk + SUBLANES, CONV_CH), F32)],
        compiler_params=_params("arbitrary", "arbitrary"),
        name="gdn",
    )(z3, z3, z3, conv_w, c0, cols, gr, z3, norm_w.reshape(1, HEAD_DIM), s0)


def _out_kernel(oa_ref, ob_ref, ma_ref, mb_ref, x_ref, p_ref, wfox_ref, wgdn_ref, wout_ref,
                wplp_ref, wplg_ref, vec_ref, o_ref, *, alpha):
    ya = jnp.dot(oa_ref[...], wfox_ref[...], preferred_element_type=F32)
    yb = jnp.dot(ob_ref[...], wgdn_ref[...], preferred_element_type=F32)
    merged = _sigmoid(ma_ref[...]) * ya + _sigmoid(mb_ref[...]) * yb
    y = alpha * x_ref[...] + _mm(merged, wout_ref[...])
    yc = y - jnp.mean(y, axis=1, keepdims=True)
    var = jnp.mean(yc * yc, axis=1, keepdims=True)
    x1 = yc * lax.rsqrt(var + LN_EPS) * vec_ref[0:1, :] + vec_ref[1:2, :]
    e = _mm(p_ref[...], wplp_ref[...])
    e = e * lax.rsqrt(jnp.mean(e * e, axis=1, keepdims=True) + RMS_EPS) * vec_ref[2:3, :]
    o_ref[...] = x1 + _sigmoid(_mm(x1, wplg_ref[...])) * e


def _out_block(oa, ob, z2d, x2d, p_all, layer, wfox, wgdn, wout, wplp, wplg, vec, alpha):
    t_rows, d = x2d.shape
    ple = p_all.shape[-1]
    tm = min(256, t_rows)
    dpw = d // WIDTH
    assert d % WIDTH == 0 and Z_M_A % dpw == 0 and Z_M_B % dpw == 0
    const = lambda shape: pl.BlockSpec((None,) + shape, lambda r: (layer,) + (0,) * len(shape),
                                       pipeline_mode=pl.Buffered(1))
    return pl.pallas_call(
        functools.partial(_out_kernel, alpha=alpha),
        grid=(t_rows // tm,),
        in_specs=[
            pl.BlockSpec((tm, WIDTH), lambda r: (r, 0)),
            pl.BlockSpec((tm, WIDTH), lambda r: (r, 0)),
            pl.BlockSpec((tm, d), lambda r: (r, Z_M_A // dpw)),
            pl.BlockSpec((tm, d), lambda r: (r, Z_M_B // dpw)),
            pl.BlockSpec((tm, d), lambda r: (r, 0)),
            pl.BlockSpec((None, tm, ple), lambda r: (layer, r, 0)),
            const((WIDTH, d)), const((WIDTH, d)), const((d, d)), const((ple, d)), const((d, d)),
            const((SUBLANES, d)),
        ],
        out_specs=pl.BlockSpec((tm, d), lambda r: (r, 0)),
        out_shape=jax.ShapeDtypeStruct((t_rows, d), F32),
        compiler_params=_params("arbitrary"),
        name="out_block",
    )(oa, ob, z2d, z2d, x2d, p_all, wfox, wgdn, wout, wplp, wplg, vec)


def _layer(x2d, bsz, n_tok, p_all, layer, wts, conv0, s0, alpha, k_all, v_all, fox_cache=None):
    lp = x2d.shape[0] // bsz
    z2d, small, k_all, v_all = _proj(x2d, wts["w_big"], wts["w_small"], wts["par"], k_all, v_all, layer)
    z3 = z2d.reshape(bsz, lp, -1)
    k4 = k_all.reshape(-1, bsz, lp, WIDTH)
    v4 = v_all.reshape(-1, bsz, lp, WIDTH)
    small3 = small.reshape(bsz, lp, LANES)
    logf_all = small3[:, :, 0:HEADS]
    logf = logf_all[:, :n_tok]
    valid = (jnp.arange(lp) < n_tok)[None, :, None]
    g = jnp.where(valid, small3[:, :, HEADS:2 * HEADS], 0.0)
    beta = jnp.where(valid, small3[:, :, 2 * HEADS:3 * HEADS], 0.0)

    to_rows = lambda a: a.transpose(0, 2, 1).reshape(bsz * HEADS, a.shape[1])
    c_rows = _cumsum_rows(to_rows(logf_all), init=None if fox_cache is None else fox_cache[3][layer])
    gc_rows = _cumsum_rows(to_rows(g), period=GDN_CHUNK)
    gc = gc_rows.reshape(bsz, HEADS, lp).transpose(0, 2, 1)
    cols = jnp.concatenate([gc, beta, jnp.zeros((bsz, lp, LANES - 2 * HEADS), F32)], axis=-1)

    if fox_cache is None:
        oa = _fox_prompt(z3, k4, v4, layer, c_rows)
    else:
        cache_k, cache_v, cp_flat, _ = fox_cache
        oa = _fox_decode(z3, k4, v4, n_tok, cache_k, cache_v, layer, cp_flat, c_rows[:, :n_tok])
        oa = jnp.pad(oa, ((0, 0), (0, lp - n_tok), (0, 0)))
    ob, s_new = _gdn(z3, wts["conv_w"], conv0, cols, gc_rows, wts["norm_w"], s0)

    x_out = _out_block(oa.reshape(bsz * lp, WIDTH), ob.reshape(bsz * lp, WIDTH), z2d, x2d, p_all, layer,
                       wts["wfox"], wts["wgdn"], wts["wout"], wts["wplp"], wts["wplg"], wts["vec"], alpha)

    new_conv = z3[:, n_tok - (CONV_W - 1):n_tok, Z_QKV_B * WIDTH:Z_QKV_B * WIDTH + CONV_CH]
    return x_out, k_all, v_all, logf, new_conv, s_new


def kernel(x_prompt, x_sample, cache_fox_k, cache_fox_v, cache_fox_logf, state_gdn_conv, state_gdn, p_prompt, p_sample, w_in, fox_f_bias, gdn_conv_w, gdn_a_log, gdn_dt_bias, gdn_norm_w, w_out_fox, w_out_gdn, w_out, ln_g, ln_b, w_pl_proj, w_pl_gate, pl_norm_w):
    depth = w_in.shape[0]
    bp, seq, d = x_prompt.shape
    bs, dseq, _ = x_sample.shape
    past = cache_fox_k.shape[2]
    ple = p_prompt.shape[-1]
    assert d % WIDTH == 0 and dseq >= CONV_W - 1 and seq >= CONV_W - 1
    alpha = (2 * depth) ** 0.25

    o_f = 3 * WIDTH
    o_ga = o_f + HEADS
    o_qkv = o_ga + WIDTH
    o_a = o_qkv + CONV_CH
    o_gb = o_a + 2 * HEADS
    w_big = jnp.concatenate([w_in[..., :WIDTH], w_in[..., o_ga:o_a], w_in[..., o_gb:],
                             w_in[..., WIDTH:o_f]], axis=-1).astype(BF16)
    w_small = jnp.concatenate([w_in[..., o_f:o_ga], w_in[..., o_a:o_gb],
                               jnp.zeros((depth, d, LANES - 3 * HEADS), F32)], axis=-1)
    w_small_hi = w_small.astype(BF16)
    w_small = jnp.stack([w_small_hi, (w_small - w_small_hi.astype(F32)).astype(BF16)], axis=1)
    zpad = jnp.zeros((depth, LANES - 2 * HEADS), F32)
    par = jnp.stack([jnp.concatenate([fox_f_bias, gdn_dt_bias, zpad], axis=-1),
                     jnp.concatenate([jnp.zeros_like(fox_f_bias), gdn_a_log, zpad], axis=-1)], axis=1)
    par = jnp.pad(par, ((0, 0), (0, SUBLANES - 2), (0, 0)))
    vec = jnp.pad(jnp.stack([ln_g, ln_b, pl_norm_w], axis=1), ((0, 0), (0, SUBLANES - 3), (0, 0)))
    wfox, wgdn, wout = w_out_fox.astype(BF16), w_out_gdn.astype(BF16), w_out.astype(BF16)
    wplp, wplg = w_pl_proj.astype(BF16), w_pl_gate.astype(BF16)

    def layer_weights(i):
        return dict(w_big=w_big, w_small=w_small, par=par, conv_w=gdn_conv_w[i],
                    norm_w=gdn_norm_w[i], wfox=wfox, wgdn=wgdn, wout=wout,
                    wplp=wplp, wplg=wplg, vec=vec)

    lp_s = -(-dseq // GDN_CHUNK) * GDN_CHUNK
    xs = jnp.pad(x_sample, ((0, 0), (0, lp_s - dseq), (0, 0))).reshape(bs * lp_s, d)
    ps_all = jnp.pad(p_sample, ((0, 0), (0, 0), (0, lp_s - dseq), (0, 0))).reshape(depth, bs * lp_s, ple)
    xp = x_prompt.reshape(bp * seq, d)
    pp_all = p_prompt.reshape(depth, bp * seq, ple)

    cp_rows = _cumsum_rows(cache_fox_logf.transpose(0, 1, 3, 2).reshape(depth * bs * HEADS, past))
    cp_rows = cp_rows.reshape(depth, bs, HEADS, past)
    cp_flat = cp_rows.transpose(0, 1, 3, 2).reshape(depth, bs, 1, past * HEADS)
    cp_last = cp_rows[..., past - 1:].reshape(depth, bs * HEADS, 1)

    conv0_p = jnp.zeros((bp, CONV_W - 1, CONV_CH), F32)
    s0_p = jnp.zeros((bp, HEADS, HEAD_DIM, HEAD_DIM), F32)
    kp = jnp.zeros((depth, bp * seq, WIDTH), F32)
    vp = jnp.zeros((depth, bp * seq, WIDTH), F32)
    ks = jnp.zeros((depth, bs * lp_s, WIDTH), F32)
    vs = jnp.zeros((depth, bs * lp_s, WIDTH), F32)
    outs_p, outs_s = [], []
    for i in range(depth):
        wts = layer_weights(i)
        xp, kp, vp, *rest_p = _layer(xp, bp, seq, pp_all, i, wts, conv0_p, s0_p, alpha, kp, vp)
        xs, ks, vs, *rest_s = _layer(xs, bs, dseq, ps_all, i, wts, state_gdn_conv[i], state_gdn[i], alpha,
                                     ks, vs, fox_cache=(cache_fox_k, cache_fox_v, cp_flat, cp_last))
        outs_p.append(rest_p)
        outs_s.append(rest_s)
    stack = lambda outs, j: jnp.stack([o[j] for o in outs])
    heads_p = lambda a: a.reshape(depth, bp, seq, HEADS, HEAD_DIM)
    heads_s = lambda a: a.reshape(depth, bs, lp_s, HEADS, HEAD_DIM)[:, :, :dseq]
    y_prompt = xp.reshape(bp, seq, d)
    y_sample = xs.reshape(bs, lp_s, d)[:, :dseq]
    return (y_prompt, y_sample,
            heads_p(kp), heads_p(vp), stack(outs_p, 0), stack(outs_p, 1), stack(outs_p, 2),
            heads_s(ks), heads_s(vs), stack(outs_s, 0), stack(outs_s, 1), stack(outs_s, 2))
```

```python
import functools
import math

import jax
import jax.numpy as jnp
from jax import lax
from jax.experimental import pallas as pl
from jax.experimental.pallas import tpu as pltpu

F32 = jnp.float32
BF16 = jnp.bfloat16

HEADS = 8
HEAD_DIM = 128
WIDTH = HEADS * HEAD_DIM
CONV_W = 4
CONV_CH = 3 * WIDTH
LN_EPS = 1e-5
RMS_EPS = 1e-6
LOG2E = math.log2(math.e)
LANES = 128
SUBLANES = 8
GDN_CHUNK = 128
GDN_BATCHES = 2
INV_BASE = 8
FOX_BQ = 1024
FOX_BK = 1024
FOX_ROWS = 256
FOX_STRIP = 16
VMEM_LIMIT = 56 * 1024 * 1024

Z_Q, Z_GATE_A, Z_QKV_B, Z_GATE_B, Z_M_A, Z_M_B, Z_END = 0, 1, 2, 5, 6, 8, 10


def _params(*sem):
    return pltpu.CompilerParams(dimension_semantics=sem, vmem_limit_bytes=VMEM_LIMIT)


def _mm(a, b):
    return jnp.dot(a.astype(BF16), b.astype(BF16), preferred_element_type=F32)


def _mm_nt(a, b):
    return lax.dot_general(a.astype(BF16), b.astype(BF16), (((1,), (1,)), ((), ())),
                           preferred_element_type=F32)


def _mm_tn(a, b):
    return lax.dot_general(a.astype(BF16), b.astype(BF16), (((0,), (0,)), ((), ())),
                           preferred_element_type=F32)


def _sigmoid(x):
    return 0.5 * jnp.tanh(0.5 * x) + 0.5


def _silu(x):
    return x * _sigmoid(x)


def _proj_kernel(x_ref, w_ref, ws_ref, par_ref, k_in_ref, v_in_ref, z_ref, s_ref, k_ref, v_ref, xb_ref,
                 *, z_tiles, kv_tiles):
    del k_in_ref, v_in_ref
    j = pl.program_id(1)

    @pl.when(j == 0)
    def _():
        x = x_ref[...]
        xb = x.astype(BF16)
        xb_ref[...] = xb
        x_lo = (x - xb.astype(F32)).astype(BF16)
        raw = (jnp.dot(xb, ws_ref[0], preferred_element_type=F32)
               + jnp.dot(x_lo, ws_ref[0], preferred_element_type=F32)
               + jnp.dot(xb, ws_ref[1], preferred_element_type=F32))
        y = raw + par_ref[0:1, :]
        neg_a = -jnp.exp(par_ref[1:2, :])
        t = jnp.log1p(jnp.exp(-jnp.abs(y)))
        logf = jnp.minimum(y, 0.0) - t
        g = neg_a * (jnp.maximum(y, 0.0) + t)
        beta = _sigmoid(raw)
        lane = lax.broadcasted_iota(jnp.int32, raw.shape, 1)
        s_ref[...] = jnp.where(lane < HEADS, logf,
                               jnp.where(lane < 2 * HEADS, g,
                                         jnp.where(lane < 3 * HEADS, beta, 0.0)))

    @pl.when(j < z_tiles)
    def _():
        z_ref[...] = jnp.dot(xb_ref[...], w_ref[...], preferred_element_type=F32)

    @pl.when((j >= z_tiles) & (j < z_tiles + kv_tiles))
    def _():
        k_ref[...] = jnp.dot(xb_ref[...], w_ref[...], preferred_element_type=F32)

    @pl.when(j >= z_tiles + kv_tiles)
    def _():
        v_ref[...] = jnp.dot(xb_ref[...], w_ref[...], preferred_element_type=F32)


def _proj(x2d, w_big, w_small, par, k_all, v_all, layer):
    t_rows, d = x2d.shape
    tm = min(1024, t_rows)
    tn = 512
    kv_tiles = WIDTH // tn
    z_tiles = w_big.shape[-1] // tn - 2 * kv_tiles
    clamp = lambda j, lo: jnp.clip(j - lo, 0, kv_tiles - 1)
    return pl.pallas_call(
        functools.partial(_proj_kernel, z_tiles=z_tiles, kv_tiles=kv_tiles),
        grid=(t_rows // tm, z_tiles + 2 * kv_tiles),
        in_specs=[
            pl.BlockSpec((tm, d), lambda i, j: (i, 0)),
            pl.BlockSpec((None, d, tn), lambda i, j: (layer, 0, j)),
            pl.BlockSpec((None, 2, d, LANES), lambda i, j: (layer, 0, 0, 0)),
            pl.BlockSpec((None, SUBLANES, LANES), lambda i, j: (layer, 0, 0)),
            pl.BlockSpec(memory_space=pl.ANY),
            pl.BlockSpec(memory_space=pl.ANY),
        ],
        out_specs=[
            pl.BlockSpec((tm, tn), lambda i, j: (i, jnp.minimum(j, z_tiles - 1))),
            pl.BlockSpec((tm, LANES), lambda i, j: (i, 0)),
            pl.BlockSpec((None, tm, tn), lambda i, j: (layer, i, clamp(j, z_tiles))),
            pl.BlockSpec((None, tm, tn), lambda i, j: (layer, i, clamp(j, z_tiles + kv_tiles))),
        ],
        out_shape=[
            jax.ShapeDtypeStruct((t_rows, z_tiles * tn), F32),
            jax.ShapeDtypeStruct((t_rows, LANES), F32),
            jax.ShapeDtypeStruct(k_all.shape, F32),
            jax.ShapeDtypeStruct(v_all.shape, F32),
        ],
        input_output_aliases={4: 2, 5: 3},
        scratch_shapes=[pltpu.VMEM((tm, d), BF16)],
        compiler_params=_params("arbitrary", "arbitrary"),
        name="proj",
    )(x2d, w_big, w_small, par, k_all, v_all)


def _cumsum_kernel(x_ref, init_ref, o_ref, carry_ref, *, blk, period):
    r = lax.broadcasted_iota(jnp.int32, (blk, blk), 0)
    c = lax.broadcasted_iota(jnp.int32, (blk, blk), 1)
    keep = r <= c
    if period is not None:
        keep = keep & ((r // period) == (c // period))
    y = jnp.dot(x_ref[...], keep.astype(F32), precision=lax.Precision.HIGHEST,
                preferred_element_type=F32)
    if period is None:
        @pl.when(pl.program_id(0) == 0)
        def _():
            carry_ref[...] = init_ref[...]

        y = y + carry_ref[...]
        carry_ref[...] = y[:, blk - 1:blk]
    o_ref[...] = y


def _cumsum_rows(x, period=None, init=None):
    rows, length = x.shape
    blk = min(512, length)
    assert length % blk == 0 and (period is None or (blk % period == 0 and init is None))
    if init is None:
        init = jnp.zeros((rows, 1), F32)
    return pl.pallas_call(
        functools.partial(_cumsum_kernel, blk=blk, period=period),
        grid=(length // blk,),
        in_specs=[pl.BlockSpec((rows, blk), lambda j: (0, j)), pl.BlockSpec((rows, 1), lambda j: (0, 0))],
        out_specs=pl.BlockSpec((rows, blk), lambda j: (0, j)),
        out_shape=jax.ShapeDtypeStruct((rows, length), F32),
        scratch_shapes=[pltpu.VMEM((rows, 1), F32)],
        compiler_params=_params("arbitrary"),
        name="cumsum",
    )(x, init)


def _fox_kernel(q_ref, k_ref, v_ref, c_ref, ga_ref, o_ref, kb_ref, vb_ref, *, bq, bk, rows, scale):
    qi = pl.program_id(2)
    length = k_ref.shape[0]
    groups = bq // rows

    @pl.when(qi == 0)
    def _():
        kb_ref[...] = k_ref[...].astype(BF16)
        vb_ref[:, :HEAD_DIM] = v_ref[...].astype(BF16)
        vb_ref[:, HEAD_DIM:] = jnp.ones((length, LANES), BF16)

    q = (q_ref[...] * (scale * LOG2E)).astype(BF16)
    qs = [q[r * rows:(r + 1) * rows] for r in range(groups)]
    def update(kj, carries, diag_offset=None):
        start = pl.multiple_of(kj * bk, bk)
        widths = {r: bk if diag_offset is None else min(bk, (r + 1) * rows - diag_offset)
                  for r in range(groups)}
        widths = {r: w for r, w in widths.items() if w > 0}
        kblks = {w: kb_ref[pl.ds(start, w), :] for w in set(widths.values())}
        cks = {w: c_ref[kj, :, 0:w] * LOG2E for w in set(widths.values())}
        ss = {r: lax.dot_general(qs[r], kblks[w], (((1,), (1,)), ((), ())), preferred_element_type=F32)
              for r, w in widths.items()}
        out = list(carries)
        for r, s in ss.items():
            m, acc, l = carries[r]
            width = widths[r]
            lane_blocks = [slice(c * LANES, (c + 1) * LANES) for c in range(width // LANES)]
            masked = diag_offset is not None and diag_offset + width - 1 > r * rows
            m_parts, a_parts, p_parts = [], [], []
            for i in range(rows // FOX_STRIP):
                sl = slice(i * FOX_STRIP, (i + 1) * FOX_STRIP)
                t = s[sl] - cks[width]
                if masked:
                    row = lax.broadcasted_iota(jnp.int32, (FOX_STRIP, width), 0)
                    col = lax.broadcasted_iota(jnp.int32, (FOX_STRIP, width), 1)
                    t = jnp.where(col + (diag_offset - r * rows - i * FOX_STRIP) <= row, t, -jnp.inf)
                mx = t[:, lane_blocks[0]]
                for c in lane_blocks[1:]:
                    mx = jnp.maximum(mx, t[:, c])
                m_new = jnp.maximum(m[sl], jnp.broadcast_to(jnp.max(mx, axis=1, keepdims=True), mx.shape))
                m_parts.append(m_new)
                a_parts.append(jnp.exp2(m[sl] - m_new))
                p_parts.append(jnp.concatenate([jnp.exp2(t[:, c] - m_new).astype(BF16) for c in lane_blocks],
                                               axis=1))
            alpha = jnp.concatenate(a_parts, axis=0)
            pv = jnp.dot(jnp.concatenate(p_parts, axis=0), vb_ref[pl.ds(start, width), :],
                         preferred_element_type=F32)
            out[r] = (jnp.concatenate(m_parts, axis=0), alpha * acc + pv[:, :HEAD_DIM],
                      alpha * l + pv[:, HEAD_DIM:])
        return tuple(out)

    init = (jnp.full((rows, LANES), -jnp.inf, F32), jnp.zeros((rows, HEAD_DIM), F32),
            jnp.zeros((rows, LANES), F32))
    per_q = bq // bk
    def past_q_block(j, cr):
        for d in range(per_q):
            cr = update(j * per_q + d, cr)
        return cr

    carries = lax.fori_loop(0, qi, past_q_block, (init,) * groups)
    for d in range(per_q):
        carries = update(qi * per_q + d, carries, diag_offset=d * bk)
    for r, (_, acc, l) in enumerate(carries):
        rs = slice(r * rows, (r + 1) * rows)
        o_ref[rs, :] = (acc / l * _silu(ga_ref[rs, :])).astype(BF16)


def _fox_prompt(z3, k_all, v_all, layer, c_rows):
    bsz, length, _ = z3.shape
    bq = min(FOX_BQ, length)
    bk = min(FOX_BK, bq)
    rows = min(FOX_ROWS, bq)
    assert length % bq == 0 and bq % bk == 0 and bq % rows == 0
    nk = length // bk
    c5 = c_rows.reshape(bsz, HEADS, nk, 1, bk)
    hpw = WIDTH // HEAD_DIM
    return pl.pallas_call(
        functools.partial(_fox_kernel, bq=bq, bk=bk, rows=rows, scale=HEAD_DIM ** -0.5),
        grid=(bsz, HEADS, length // bq),
        in_specs=[
            pl.BlockSpec((None, bq, HEAD_DIM), lambda b, h, i: (b, i, Z_Q * hpw + h)),
            pl.BlockSpec((None, None, length, HEAD_DIM), lambda b, h, i: (layer, b, 0, h)),
            pl.BlockSpec((None, None, length, HEAD_DIM), lambda b, h, i: (layer, b, 0, h)),
            pl.BlockSpec((None, None, nk, 1, bk), lambda b, h, i: (b, h, 0, 0, 0)),
            pl.BlockSpec((None, bq, HEAD_DIM), lambda b, h, i: (b, i, Z_GATE_A * hpw + h)),
        ],
        out_specs=pl.BlockSpec((None, bq, HEAD_DIM), lambda b, h, i: (b, i, h)),
        out_shape=jax.ShapeDtypeStruct((bsz, length, WIDTH), BF16),
        scratch_shapes=[pltpu.VMEM((length, HEAD_DIM), BF16), pltpu.VMEM((length, 2 * HEAD_DIM), BF16)],
        compiler_params=_params("arbitrary", "arbitrary", "arbitrary"),
        name="fox_prompt",
    )(z3, k_all, v_all, c5, z3)


def _fox_decode_kernel(q_ref, kn_ref, vn_ref, cn_ref, ga_ref, kp_ref, vp_ref, cp_ref, o_ref,
                       m_ref, acc_ref, l_ref, *, n, scale):
    j = pl.program_id(1)
    nrow = q_ref.shape[0]
    q = (q_ref[...] * (scale * LOG2E)).astype(BF16)

    @pl.when(j == 0)
    def _():
        m_ref[...] = jnp.full(m_ref.shape, -jnp.inf, F32)
        acc_ref[...] = jnp.zeros(acc_ref.shape, F32)
        l_ref[...] = jnp.zeros(l_ref.shape, F32)

    def update(kblk, vblk, c_row, visible):
        t = jnp.where(visible, _mm_nt(q, kblk) - c_row * LOG2E, -jnp.inf)
        m = m_ref[...]
        m_new = jnp.maximum(m, jnp.broadcast_to(jnp.max(t, axis=1, keepdims=True), m.shape))
        alpha = jnp.exp2(m - m_new)
        p = jnp.concatenate([jnp.exp2(t[:, c * LANES:(c + 1) * LANES] - m_new)
                             for c in range(t.shape[1] // LANES)], axis=1)
        m_ref[...] = m_new
        l_ref[...] = alpha * l_ref[...] + jnp.broadcast_to(jnp.sum(p, axis=1, keepdims=True), m.shape)
        acc_ref[...] = alpha * acc_ref[...] + _mm(p, vblk)

    cols = kp_ref.shape[0]
    row_head = lax.broadcasted_iota(jnp.int32, (nrow, cols), 0) // n
    col_head = lax.broadcasted_iota(jnp.int32, (nrow, cols), 1) % HEADS
    update(kp_ref[...], vp_ref[...], cp_ref[...], row_head == col_head)

    @pl.when(j == pl.num_programs(1) - 1)
    def _():
        r = lax.broadcasted_iota(jnp.int32, (nrow, nrow), 0)
        c = lax.broadcasted_iota(jnp.int32, (nrow, nrow), 1)
        update(kn_ref[...], vn_ref[...], cn_ref[...], (r // n == c // n) & (c % n <= r % n))
        o_ref[...] = (acc_ref[...] / l_ref[...] * _silu(ga_ref[...])).astype(BF16)


def _fox_decode(z3, k4, v4, n_tok, cache_k, cache_v, layer, cp_flat, cn_rows):
    bsz = z3.shape[0]
    past = cache_k.shape[2]
    nrow = HEADS * n_tok
    pc = min(1024, past)
    assert past % pc == 0 and nrow % LANES == 0

    def head_rows(a):
        return a.reshape(bsz, n_tok, HEADS, HEAD_DIM).transpose(0, 2, 1, 3).reshape(bsz, nrow, HEAD_DIM)

    seg = lambda s: z3[:, :n_tok, s * WIDTH:(s + 1) * WIDTH]
    k2 = cache_k.reshape(cache_k.shape[:2] + (past * HEADS, HEAD_DIM))
    v2 = cache_v.reshape(cache_v.shape[:2] + (past * HEADS, HEAD_DIM))
    rows_blk = pl.BlockSpec((None, nrow, HEAD_DIM), lambda b, j: (b, 0, 0))
    cache_blk = pl.BlockSpec((None, None, pc * HEADS, HEAD_DIM), lambda b, j: (layer, b, j, 0))
    o_rows = pl.pallas_call(
        functools.partial(_fox_decode_kernel, n=n_tok, scale=HEAD_DIM ** -0.5),
        grid=(bsz, past // pc),
        in_specs=[
            rows_blk, rows_blk, rows_blk,
            pl.BlockSpec((None, 1, nrow), lambda b, j: (b, 0, 0)),
            rows_blk, cache_blk, cache_blk,
            pl.BlockSpec((None, None, 1, pc * HEADS), lambda b, j: (layer, b, 0, j)),
        ],
        out_specs=rows_blk,
        out_shape=jax.ShapeDtypeStruct((bsz, nrow, HEAD_DIM), BF16),
        scratch_shapes=[pltpu.VMEM((nrow, LANES), F32), pltpu.VMEM((nrow, HEAD_DIM), F32),
                        pltpu.VMEM((nrow, LANES), F32)],
        compiler_params=_params("arbitrary", "arbitrary"),
        name="fox_decode",
    )(head_rows(seg(Z_Q)), head_rows(k4[layer, :, :n_tok]), head_rows(v4[layer, :, :n_tok]),
      cn_rows.reshape(bsz, 1, nrow), head_rows(seg(Z_GATE_A)), k2, v2, cp_flat)
    return o_rows.reshape(bsz, HEADS, n_tok, HEAD_DIM).transpose(0, 2, 1, 3).reshape(bsz, n_tok, WIDTH)


def _gdn_kernel(qp_ref, kp_ref, vp_ref, cw_ref, c0_ref, cols_ref, gr_ref, gb_ref, nw_ref, s0_ref,
                og_ref, s_ref, xp_ref, *, chunk, scale):
    pad = SUBLANES
    nb = qp_ref.shape[0]

    @pl.when(pl.program_id(1) == 0)
    def _():
        s_ref[...] = s0_ref[...]
        xp_ref[:, 0:pad, :] = c0_ref[...]

    convs = []
    for bb in range(nb):
        xp_ref[bb, pad:pad + chunk, 0:WIDTH] = qp_ref[bb]
        xp_ref[bb, pad:pad + chunk, WIDTH:2 * WIDTH] = kp_ref[bb]
        xp_ref[bb, pad:pad + chunk, 2 * WIDTH:3 * WIDTH] = vp_ref[bb]
        xp = xp_ref[bb]
        acc = xp * cw_ref[0:1, :]
        for i in range(1, CONV_W):
            acc = pltpu.roll(acc, 1, axis=0) + xp * cw_ref[i:i + 1, :]
        convs.append(_silu(acc[pad:pad + chunk]))
        xp_ref[bb, 0:pad, :] = xp_ref[bb, chunk:chunk + pad, :]

    row = lax.broadcasted_iota(jnp.int32, (chunk, chunk), 0)
    col = lax.broadcasted_iota(jnp.int32, (chunk, chunk), 1)
    incl = row >= col
    strict = row > col
    sizes = [INV_BASE << i for i in range(int(math.log2(chunk // INV_BASE)) + 1)]
    same = [(row >> int(math.log2(b))) == (col >> int(math.log2(b))) for b in sizes[:-1]]
    diag_mask = strict & same[0]
    off_masks = [strict & ~same[i] & (same[i + 1] if i + 1 < len(same) else True)
                 for i in range(len(same))]

    units = [(bb, hh) for bb in range(nb) for hh in range(HEADS)]
    heads = range(len(units))
    sl = lambda seg, hh: slice(seg * WIDTH + hh * HEAD_DIM, seg * WIDTH + (hh + 1) * HEAD_DIM)
    q, k, v, gc, beta, e, lmat, qk = ([None] * len(units) for _ in range(8))
    for h, (bb, hh) in enumerate(units):
        qh, kh = convs[bb][:, sl(0, hh)], convs[bb][:, sl(1, hh)]
        q[h] = qh * (lax.rsqrt(jnp.sum(qh * qh, axis=1, keepdims=True) + RMS_EPS) * scale)
        k[h] = kh * lax.rsqrt(jnp.sum(kh * kh, axis=1, keepdims=True) + RMS_EPS)
        v[h] = convs[bb][:, sl(2, hh)]
        gc[h] = cols_ref[bb, :, hh:hh + 1]
        beta[h] = cols_ref[bb, :, HEADS + hh:HEADS + hh + 1]
        e[h] = jnp.exp(gc[h])
    for h, (bb, hh) in enumerate(units):
        decay = jnp.exp(jnp.where(incl, gc[h] - gr_ref[bb, hh:hh + 1, :], -jnp.inf))
        kq = _mm_nt(jnp.concatenate([k[h] * beta[h], q[h]], axis=0), k[h])
        lmat[h] = jnp.where(strict, kq[:chunk] * decay, 0.0)
        qk[h] = jnp.where(incl, kq[chunk:] * decay, 0.0)
    xpow = [jnp.where(diag_mask, lmat[h], 0.0) for h in heads]
    nm = [-xpow[h] for h in heads]
    for _ in range(int(math.log2(INV_BASE)) - 1):
        xpow = [_mm(xpow[h], xpow[h]) for h in heads]
        nm = [nm[h] + xpow[h] + _mm(nm[h], xpow[h]) for h in heads]
    for off in off_masks:
        loff = [jnp.where(off, lmat[h], 0.0) for h in heads]
        t_loff = [loff[h] + _mm(nm[h], loff[h]) for h in heads]
        nm = [nm[h] - (t_loff[h] + _mm(t_loff[h], nm[h])) for h in heads]
    uw = []
    for h in heads:
        rhs = jnp.concatenate([v[h] * beta[h], k[h] * (beta[h] * e[h])], axis=1)
        uw.append(rhs + _mm(nm[h], rhs))
    s_old = [s_ref[bb, hh] for bb, hh in units]
    ws_qs = [_mm(jnp.concatenate([uw[h][:, HEAD_DIM:], q[h] * e[h]], axis=0), s_old[h]) for h in heads]
    v_new = [uw[h][:, :HEAD_DIM] - ws_qs[h][:chunk] for h in heads]
    for h, (bb, hh) in enumerate(units):
        g_last = gc[h][chunk - 1:chunk, :]
        k_tail = k[h] * jnp.exp(g_last - gc[h])
        s_ref[bb, hh] = s_old[h] * jnp.exp(g_last) + _mm_tn(k_tail, v_new[h])
    for h, (bb, hh) in enumerate(units):
        o = ws_qs[h][chunk:] + _mm(qk[h], v_new[h])
        o = o * lax.rsqrt(jnp.mean(o * o, axis=1, keepdims=True) + RMS_EPS) * nw_ref[...]
        og_ref[bb, :, sl(0, hh)] = (o * _silu(gb_ref[bb, :, sl(0, hh)])).astype(BF16)


def _gdn(z3, conv_w, conv0, cols, gc_rows, norm_w, s0):
    bsz, length, _ = z3.shape
    chunk = GDN_CHUNK
    assert length % chunk == 0
    nc = length // chunk
    nb = GDN_BATCHES if bsz % GDN_BATCHES == 0 else 1
    gr = gc_rows.reshape(bsz, HEADS, nc, chunk).transpose(0, 2, 1, 3)
    c0 = jnp.pad(conv0, ((0, 0), (SUBLANES - (CONV_W - 1), 0), (0, 0)))
    zblk = lambda seg: pl.BlockSpec((nb, chunk, WIDTH), lambda b, l: (b, l, seg))
    return pl.pallas_call(
        functools.partial(_gdn_kernel, chunk=chunk, scale=HEAD_DIM ** -0.5),
        grid=(bsz // nb, nc),
        in_specs=[
            zblk(Z_QKV_B), zblk(Z_QKV_B + 1), zblk(Z_QKV_B + 2),
            pl.BlockSpec((CONV_W, CONV_CH), lambda b, l: (0, 0)),
            pl.BlockSpec((nb, SUBLANES, CONV_CH), lambda b, l: (b, 0, 0)),
            pl.BlockSpec((nb, chunk, LANES), lambda b, l: (b, l, 0)),
            pl.BlockSpec((nb, None, HEADS, chunk), lambda b, l: (b, l, 0, 0)),
            zblk(Z_GATE_B),
            pl.BlockSpec((1, HEAD_DIM), lambda b, l: (0, 0)),
            pl.BlockSpec((nb, HEADS, HEAD_DIM, HEAD_DIM), lambda b, l: (b, 0, 0, 0)),
        ],
        out_specs=[
            pl.BlockSpec((nb, chunk, WIDTH), lambda b, l: (b, l, 0)),
            pl.BlockSpec((nb, HEADS, HEAD_DIM, HEAD_DIM), lambda b, l: (b, 0, 0, 0)),
        ],
        out_shape=[
            jax.ShapeDtypeStruct((bsz, length, WIDTH), BF16),
            jax.ShapeDtypeStruct((bsz, HEADS, HEAD_DIM, HEAD_DIM), F32),
        ],
        scratch_shapes=[pltpu.VMEM((nb, chunk + SUBLANES, CONV_CH), F32)],
        compiler_params=_params("arbitrary", "arbitrary"),
        name="gdn",
    )(z3, z3, z3, conv_w, c0, cols, gr, z3, norm_w.reshape(1, HEAD_DIM), s0)


def _out_kernel(oa_ref, ob_ref, ma_ref, mb_ref, x_ref, p_ref, wfox_ref, wgdn_ref, wout_ref,
                wplp_ref, wplg_ref, vec_ref, o_ref, *, alpha):
    ya = jnp.dot(oa_ref[...], wfox_ref[...], preferred_element_type=F32)
    yb = jnp.dot(ob_ref[...], wgdn_ref[...], preferred_element_type=F32)
    merged = _sigmoid(ma_ref[...]) * ya + _sigmoid(mb_ref[...]) * yb
    y = alpha * x_ref[...] + _mm(merged, wout_ref[...])
    yc = y - jnp.mean(y, axis=1, keepdims=True)
    var = jnp.mean(yc * yc, axis=1, keepdims=True)
    x1 = yc * lax.rsqrt(var + LN_EPS) * vec_ref[0:1, :] + vec_ref[1:2, :]
    e = _mm(p_ref[...], wplp_ref[...])
    e = e * lax.rsqrt(jnp.mean(e * e, axis=1, keepdims=True) + RMS_EPS) * vec_ref[2:3, :]
    o_ref[...] = x1 + _sigmoid(_mm(x1, wplg_ref[...])) * e


def _out_block(oa, ob, z2d, x2d, p_all, layer, wfox, wgdn, wout, wplp, wplg, vec, alpha):
    t_rows, d = x2d.shape
    ple = p_all.shape[-1]
    tm = min(256, t_rows)
    dpw = d // WIDTH
    assert d % WIDTH == 0 and Z_M_A % dpw == 0 and Z_M_B % dpw == 0
    const = lambda shape: pl.BlockSpec((None,) + shape, lambda r: (layer,) + (0,) * len(shape),
                                       pipeline_mode=pl.Buffered(1))
    return pl.pallas_call(
        functools.partial(_out_kernel, alpha=alpha),
        grid=(t_rows // tm,),
        in_specs=[
            pl.BlockSpec((tm, WIDTH), lambda r: (r, 0)),
            pl.BlockSpec((tm, WIDTH), lambda r: (r, 0)),
            pl.BlockSpec((tm, d), lambda r: (r, Z_M_A // dpw)),
            pl.BlockSpec((tm, d), lambda r: (r, Z_M_B // dpw)),
            pl.BlockSpec((tm, d), lambda r: (r, 0)),
            pl.BlockSpec((None, tm, ple), lambda r: (layer, r, 0)),
            const((WIDTH, d)), const((WIDTH, d)), const((d, d)), const((ple, d)), const((d, d)),
            const((SUBLANES, d)),
        ],
        out_specs=pl.BlockSpec((tm, d), lambda r: (r, 0)),
        out_shape=jax.ShapeDtypeStruct((t_rows, d), F32),
        compiler_params=_params("arbitrary"),
        name="out_block",
    )(oa, ob, z2d, z2d, x2d, p_all, wfox, wgdn, wout, wplp, wplg, vec)


def _layer(x2d, bsz, n_tok, p_all, layer, wts, conv0, s0, alpha, k_all, v_all, fox_cache=None):
    lp = x2d.shape[0] // bsz
    z2d, small, k_all, v_all = _proj(x2d, wts["w_big"], wts["w_small"], wts["par"], k_all, v_all, layer)
    z3 = z2d.reshape(bsz, lp, -1)
    k4 = k_all.reshape(-1, bsz, lp, WIDTH)
    v4 = v_all.reshape(-1, bsz, lp, WIDTH)
    small3 = small.reshape(bsz, lp, LANES)
    logf_all = small3[:, :, 0:HEADS]
    logf = logf_all[:, :n_tok]
    valid = (jnp.arange(lp) < n_tok)[None, :, None]
    g = jnp.where(valid, small3[:, :, HEADS:2 * HEADS], 0.0)
    beta = jnp.where(valid, small3[:, :, 2 * HEADS:3 * HEADS], 0.0)

    to_rows = lambda a: a.transpose(0, 2, 1).reshape(bsz * HEADS, a.shape[1])
    c_rows = _cumsum_rows(to_rows(logf_all), init=None if fox_cache is None else fox_cache[3][layer])
    gc_rows = _cumsum_rows(to_rows(g), period=GDN_CHUNK)
    gc = gc_rows.reshape(bsz, HEADS, lp).transpose(0, 2, 1)
    cols = jnp.concatenate([gc, beta, jnp.zeros((bsz, lp, LANES - 2 * HEADS), F32)], axis=-1)

    if fox_cache is None:
        oa = _fox_prompt(z3, k4, v4, layer, c_rows)
    else:
        cache_k, cache_v, cp_flat, _ = fox_cache
        oa = _fox_decode(z3, k4, v4, n_tok, cache_k, cache_v, layer, cp_flat, c_rows[:, :n_tok])
        oa = jnp.pad(oa, ((0, 0), (0, lp - n_tok), (0, 0)))
    ob, s_new = _gdn(z3, wts["conv_w"], conv0, cols, gc_rows, wts["norm_w"], s0)

    x_out = _out_block(oa.reshape(bsz * lp, WIDTH), ob.reshape(bsz * lp, WIDTH), z2d, x2d, p_all, layer,
                       wts["wfox"], wts["wgdn"], wts["wout"], wts["wplp"], wts["wplg"], wts["vec"], alpha)

    new_conv = z3[:, n_tok - (CONV_W - 1):n_tok, Z_QKV_B * WIDTH:Z_QKV_B * WIDTH + CONV_CH]
    return x_out, k_all, v_all, logf, new_conv, s_new


def kernel(x_prompt, x_sample, cache_fox_k, cache_fox_v, cache_fox_logf, state_gdn_conv, state_gdn, p_prompt, p_sample, w_in, fox_f_bias, gdn_conv_w, gdn_a_log, gdn_dt_bias, gdn_norm_w, w_out_fox, w_out_gdn, w_out, ln_g, ln_b, w_pl_proj, w_pl_gate, pl_norm_w):
    depth = w_in.shape[0]
    bp, seq, d = x_prompt.shape
    bs, dseq, _ = x_sample.shape
    past = cache_fox_k.shape[2]
    ple = p_prompt.shape[-1]
    assert d % WIDTH == 0 and dseq >= CONV_W - 1 and seq >= CONV_W - 1
    alpha = (2 * depth) ** 0.25

    o_f = 3 * WIDTH
    o_ga = o_f + HEADS
    o_qkv = o_ga + WIDTH
    o_a = o_qkv + CONV_CH
    o_gb = o_a + 2 * HEADS
    w_big = jnp.concatenate([w_in[..., :WIDTH], w_in[..., o_ga:o_a], w_in[..., o_gb:],
                             w_in[..., WIDTH:o_f]], axis=-1).astype(BF16)
    w_small = jnp.concatenate([w_in[..., o_f:o_ga], w_in[..., o_a:o_gb],
                               jnp.zeros((depth, d, LANES - 3 * HEADS), F32)], axis=-1)
    w_small_hi = w_small.astype(BF16)
    w_small = jnp.stack([w_small_hi, (w_small - w_small_hi.astype(F32)).astype(BF16)], axis=1)
    zpad = jnp.zeros((depth, LANES - 2 * HEADS), F32)
    par = jnp.stack([jnp.concatenate([fox_f_bias, gdn_dt_bias, zpad], axis=-1),
                     jnp.concatenate([jnp.zeros_like(fox_f_bias), gdn_a_log, zpad], axis=-1)], axis=1)
    par = jnp.pad(par, ((0, 0), (0, SUBLANES - 2), (0, 0)))
    vec = jnp.pad(jnp.stack([ln_g, ln_b, pl_norm_w], axis=1), ((0, 0), (0, SUBLANES - 3), (0, 0)))
    wfox, wgdn, wout = w_out_fox.astype(BF16), w_out_gdn.astype(BF16), w_out.astype(BF16)
    wplp, wplg = w_pl_proj.astype(BF16), w_pl_gate.astype(BF16)

    def layer_weights(i):
        return dict(w_big=w_big, w_small=w_small, par=par, conv_w=gdn_conv_w[i],
                    norm_w=gdn_norm_w[i], wfox=wfox, wgdn=wgdn, wout=wout,
                    wplp=wplp, wplg=wplg, vec=vec)

    lp_s = -(-dseq // GDN_CHUNK) * GDN_CHUNK
    xs = jnp.pad(x_sample, ((0, 0), (0, lp_s - dseq), (0, 0))).reshape(bs * lp_s, d)
    ps_all = jnp.pad(p_sample, ((0, 0), (0, 0), (0, lp_s - dseq), (0, 0))).reshape(depth, bs * lp_s, ple)
    xp = x_prompt.reshape(bp * seq, d)
    pp_all = p_prompt.reshape(depth, bp * seq, ple)

    cp_rows = _cumsum_rows(cache_fox_logf.transpose(0, 1, 3, 2).reshape(depth * bs * HEADS, past))
    cp_rows = cp_rows.reshape(depth, bs, HEADS, past)
    cp_flat = cp_rows.transpose(0, 1, 3, 2).reshape(depth, bs, 1, past * HEADS)
    cp_last = cp_rows[..., past - 1:].reshape(depth, bs * HEADS, 1)

    conv0_p = jnp.zeros((bp, CONV_W - 1, CONV_CH), F32)
    s0_p = jnp.zeros((bp, HEADS, HEAD_DIM, HEAD_DIM), F32)
    kp = lax.empty((depth, bp * seq, WIDTH), F32)
    vp = lax.empty((depth, bp * seq, WIDTH), F32)
    ks = lax.empty((depth, bs * lp_s, WIDTH), F32)
    vs = lax.empty((depth, bs * lp_s, WIDTH), F32)
    outs_p, outs_s = [], []
    for i in range(depth):
        wts = layer_weights(i)
        xp, kp, vp, *rest_p = _layer(xp, bp, seq, pp_all, i, wts, conv0_p, s0_p, alpha, kp, vp)
        xs, ks, vs, *rest_s = _layer(xs, bs, dseq, ps_all, i, wts, state_gdn_conv[i], state_gdn[i], alpha,
                                     ks, vs, fox_cache=(cache_fox_k, cache_fox_v, cp_flat, cp_last))
        outs_p.append(rest_p)
        outs_s.append(rest_s)
    stack = lambda outs, j: jnp.stack([o[j] for o in outs])
    heads_p = lambda a: a.reshape(depth, bp, seq, HEADS, HEAD_DIM)
    heads_s = lambda a: a.reshape(depth, bs, lp_s, HEADS, HEAD_DIM)[:, :, :dseq]
    y_prompt = xp.reshape(bp, seq, d)
    y_sample = xs.reshape(bs, lp_s, d)[:, :dseq]
    return (y_prompt, y_sample,
            heads_p(kp), heads_p(vp), stack(outs_p, 0), stack(outs_p, 1), stack(outs_p, 2),
            heads_s(ks), heads_s(vs), stack(outs_s, 0), stack(outs_s, 1), stack(outs_s, 2))
```

```python
import functools
import math

import jax
import jax.numpy as jnp
from jax import lax
from jax.experimental import pallas as pl
from jax.experimental.pallas import tpu as pltpu

F32 = jnp.float32
BF16 = jnp.bfloat16

HEADS = 8
HEAD_DIM = 128
WIDTH = HEADS * HEAD_DIM
CONV_W = 4
CONV_CH = 3 * WIDTH
LN_EPS = 1e-5
RMS_EPS = 1e-6
LOG2E = math.log2(math.e)
LANES = 128
SUBLANES = 8
GDN_CHUNK = 128
GDN_BATCHES = 2
INV_BASE = 8
FOX_BQ = 1024
FOX_BK = 1024
FOX_ROWS = 256
FOX_HEADS_PER_STEP = 2
FOX_STRIP = 16
VMEM_LIMIT = 56 * 1024 * 1024

Z_Q, Z_GATE_A, Z_QKV_B, Z_GATE_B, Z_M_A, Z_M_B, Z_END = 0, 1, 2, 5, 6, 8, 10


def _params(*sem):
    return pltpu.CompilerParams(dimension_semantics=sem, vmem_limit_bytes=VMEM_LIMIT)


def _mm(a, b):
    return jnp.dot(a.astype(BF16), b.astype(BF16), preferred_element_type=F32)


def _mm_nt(a, b):
    return lax.dot_general(a.astype(BF16), b.astype(BF16), (((1,), (1,)), ((), ())),
                           preferred_element_type=F32)


def _mm_tn(a, b):
    return lax.dot_general(a.astype(BF16), b.astype(BF16), (((0,), (0,)), ((), ())),
                           preferred_element_type=F32)


def _sigmoid(x):
    return 0.5 * jnp.tanh(0.5 * x) + 0.5


def _silu(x):
    return x * _sigmoid(x)


def _proj_kernel(x_ref, w_ref, ws_ref, par_ref, k_in_ref, v_in_ref, z_ref, s_ref, k_ref, v_ref, xb_ref,
                 *, z_tiles, kv_tiles):
    del k_in_ref, v_in_ref
    j = pl.program_id(1)

    @pl.when(j == 0)
    def _():
        x = x_ref[...]
        xb = x.astype(BF16)
        xb_ref[...] = xb
        x_lo = (x - xb.astype(F32)).astype(BF16)
        raw = (jnp.dot(xb, ws_ref[0], preferred_element_type=F32)
               + jnp.dot(x_lo, ws_ref[0], preferred_element_type=F32)
               + jnp.dot(xb, ws_ref[1], preferred_element_type=F32))
        y = raw + par_ref[0:1, :]
        neg_a = -jnp.exp(par_ref[1:2, :])
        t = jnp.log1p(jnp.exp(-jnp.abs(y)))
        logf = jnp.minimum(y, 0.0) - t
        g = neg_a * (jnp.maximum(y, 0.0) + t)
        beta = _sigmoid(raw)
        lane = lax.broadcasted_iota(jnp.int32, raw.shape, 1)
        s_ref[...] = jnp.where(lane < HEADS, logf,
                               jnp.where(lane < 2 * HEADS, g,
                                         jnp.where(lane < 3 * HEADS, beta, 0.0)))

    @pl.when(j < z_tiles)
    def _():
        z_ref[...] = jnp.dot(xb_ref[...], w_ref[...], preferred_element_type=F32)

    @pl.when((j >= z_tiles) & (j < z_tiles + kv_tiles))
    def _():
        k_ref[...] = jnp.dot(xb_ref[...], w_ref[...], preferred_element_type=F32)

    @pl.when(j >= z_tiles + kv_tiles)
    def _():
        v_ref[...] = jnp.dot(xb_ref[...], w_ref[...], preferred_element_type=F32)


def _proj(x2d, w_big, w_small, par, k_all, v_all, layer):
    t_rows, d = x2d.shape
    tm = min(1024, t_rows)
    tn = 512
    kv_tiles = WIDTH // tn
    z_tiles = w_big.shape[-1] // tn - 2 * kv_tiles
    clamp = lambda j, lo: jnp.clip(j - lo, 0, kv_tiles - 1)
    return pl.pallas_call(
        functools.partial(_proj_kernel, z_tiles=z_tiles, kv_tiles=kv_tiles),
        grid=(t_rows // tm, z_tiles + 2 * kv_tiles),
        in_specs=[
            pl.BlockSpec((tm, d), lambda i, j: (i, 0)),
            pl.BlockSpec((None, d, tn), lambda i, j: (layer, 0, j)),
            pl.BlockSpec((None, 2, d, LANES), lambda i, j: (layer, 0, 0, 0)),
            pl.BlockSpec((None, SUBLANES, LANES), lambda i, j: (layer, 0, 0)),
            pl.BlockSpec(memory_space=pl.ANY),
            pl.BlockSpec(memory_space=pl.ANY),
        ],
        out_specs=[
            pl.BlockSpec((tm, tn), lambda i, j: (i, jnp.minimum(j, z_tiles - 1))),
            pl.BlockSpec((tm, LANES), lambda i, j: (i, 0)),
            pl.BlockSpec((None, tm, tn), lambda i, j: (layer, i, clamp(j, z_tiles))),
            pl.BlockSpec((None, tm, tn), lambda i, j: (layer, i, clamp(j, z_tiles + kv_tiles))),
        ],
        out_shape=[
            jax.ShapeDtypeStruct((t_rows, z_tiles * tn), F32),
            jax.ShapeDtypeStruct((t_rows, LANES), F32),
            jax.ShapeDtypeStruct(k_all.shape, F32),
            jax.ShapeDtypeStruct(v_all.shape, F32),
        ],
        input_output_aliases={4: 2, 5: 3},
        scratch_shapes=[pltpu.VMEM((tm, d), BF16)],
        compiler_params=_params("arbitrary", "arbitrary"),
        name="proj",
    )(x2d, w_big, w_small, par, k_all, v_all)


def _cumsum_kernel(x_ref, init_ref, o_ref, carry_ref, *, blk, period):
    r = lax.broadcasted_iota(jnp.int32, (blk, blk), 0)
    c = lax.broadcasted_iota(jnp.int32, (blk, blk), 1)
    keep = r <= c
    if period is not None:
        keep = keep & ((r // period) == (c // period))
    y = jnp.dot(x_ref[...], keep.astype(F32), precision=lax.Precision.HIGHEST,
                preferred_element_type=F32)
    if period is None:
        @pl.when(pl.program_id(0) == 0)
        def _():
            carry_ref[...] = init_ref[...]

        y = y + carry_ref[...]
        carry_ref[...] = y[:, blk - 1:blk]
    o_ref[...] = y


def _cumsum_rows(x, period=None, init=None):
    rows, length = x.shape
    blk = min(512, length)
    assert length % blk == 0 and (period is None or (blk % period == 0 and init is None))
    if init is None:
        init = jnp.zeros((rows, 1), F32)
    return pl.pallas_call(
        functools.partial(_cumsum_kernel, blk=blk, period=period),
        grid=(length // blk,),
        in_specs=[pl.BlockSpec((rows, blk), lambda j: (0, j)), pl.BlockSpec((rows, 1), lambda j: (0, 0))],
        out_specs=pl.BlockSpec((rows, blk), lambda j: (0, j)),
        out_shape=jax.ShapeDtypeStruct((rows, length), F32),
        scratch_shapes=[pltpu.VMEM((rows, 1), F32)],
        compiler_params=_params("arbitrary"),
        name="cumsum",
    )(x, init)


def _fox_kernel(q_ref, k_ref, v_ref, c_ref, ga_ref, o_ref, kb_ref, vb_ref, *, bq, bk, rows, scale):
    qi = pl.program_id(2)
    length = k_ref.shape[0]
    nh = kb_ref.shape[0]
    head = lambda hh: slice(hh * HEAD_DIM, (hh + 1) * HEAD_DIM)
    units = [(hh, r * rows) for hh in range(nh) for r in range(bq // rows)]

    @pl.when(qi == 0)
    def _():
        for hh in range(nh):
            kb_ref[hh] = k_ref[:, head(hh)].astype(BF16)
            vb_ref[hh, :, :HEAD_DIM] = v_ref[:, head(hh)].astype(BF16)
            vb_ref[hh, :, HEAD_DIM:] = jnp.ones((length, LANES), BF16)

    q = (q_ref[...] * (scale * LOG2E)).astype(BF16)
    qs = [q[r0:r0 + rows, head(hh)] for hh, r0 in units]

    def update(kj, carries, diag_offset=None):
        start = pl.multiple_of(kj * bk, bk)
        widths = {u: bk if diag_offset is None else min(bk, r0 + rows - diag_offset)
                  for u, (_, r0) in enumerate(units)}
        widths = {u: w for u, w in widths.items() if w > 0}
        keyed = sorted({(units[u][0], w) for u, w in widths.items()})
        kblks = {(hh, w): kb_ref[hh, pl.ds(start, w), :] for hh, w in keyed}
        cks = {(hh, w): c_ref[hh, kj, :, 0:w] * LOG2E for hh, w in keyed}
        ss = {u: lax.dot_general(qs[u], kblks[units[u][0], w], (((1,), (1,)), ((), ())),
                                 preferred_element_type=F32)
              for u, w in widths.items()}
        out = list(carries)
        for u, s in ss.items():
            m, acc, l = carries[u]
            hh, r0 = units[u]
            width = widths[u]
            lane_blocks = [slice(c * LANES, (c + 1) * LANES) for c in range(width // LANES)]
            masked = diag_offset is not None and diag_offset + width - 1 > r0
            m_parts, a_parts, p_parts = [], [], []
            for i in range(rows // FOX_STRIP):
                sl = slice(i * FOX_STRIP, (i + 1) * FOX_STRIP)
                t = s[sl] - cks[hh, width]
                if masked:
                    row = lax.broadcasted_iota(jnp.int32, (FOX_STRIP, width), 0)
                    col = lax.broadcasted_iota(jnp.int32, (FOX_STRIP, width), 1)
                    t = jnp.where(col + (diag_offset - r0 - i * FOX_STRIP) <= row, t, -jnp.inf)
                mx = t[:, lane_blocks[0]]
                for c in lane_blocks[1:]:
                    mx = jnp.maximum(mx, t[:, c])
                m_new = jnp.maximum(m[sl], jnp.broadcast_to(jnp.max(mx, axis=1, keepdims=True), mx.shape))
                m_parts.append(m_new)
                a_parts.append(jnp.exp2(m[sl] - m_new))
                p_parts.append(jnp.concatenate([jnp.exp2(t[:, c] - m_new).astype(BF16) for c in lane_blocks],
                                               axis=1))
            alpha = jnp.concatenate(a_parts, axis=0)
            pv = jnp.dot(jnp.concatenate(p_parts, axis=0), vb_ref[hh, pl.ds(start, width), :],
                         preferred_element_type=F32)
            out[u] = (jnp.concatenate(m_parts, axis=0), alpha * acc + pv[:, :HEAD_DIM],
                      alpha * l + pv[:, HEAD_DIM:])
        return tuple(out)

    init = (jnp.full((rows, LANES), -jnp.inf, F32), jnp.zeros((rows, HEAD_DIM), F32),
            jnp.zeros((rows, LANES), F32))
    per_q = bq // bk
    def past_q_block(j, cr):
        for d in range(per_q):
            cr = update(j * per_q + d, cr)
        return cr

    carries = lax.fori_loop(0, qi, past_q_block, (init,) * len(units))
    for d in range(per_q):
        carries = update(qi * per_q + d, carries, diag_offset=d * bk)
    for (hh, r0), (_, acc, l) in zip(units, carries):
        rs = slice(r0, r0 + rows)
        o_ref[rs, head(hh)] = (acc / l * _silu(ga_ref[rs, head(hh)])).astype(BF16)


def _fox_prompt(z3, k_all, v_all, layer, c_rows):
    bsz, length, _ = z3.shape
    bq = min(FOX_BQ, length)
    bk = min(FOX_BK, bq)
    rows = min(FOX_ROWS, bq)
    assert length % bq == 0 and bq % bk == 0 and bq % rows == 0
    nk = length // bk
    c5 = c_rows.reshape(bsz, HEADS, nk, 1, bk)
    nh = FOX_HEADS_PER_STEP
    hw = nh * HEAD_DIM
    assert HEADS % nh == 0 and WIDTH % hw == 0
    seg = lambda s: s * (WIDTH // hw)
    return pl.pallas_call(
        functools.partial(_fox_kernel, bq=bq, bk=bk, rows=rows, scale=HEAD_DIM ** -0.5),
        grid=(bsz, HEADS // nh, length // bq),
        in_specs=[
            pl.BlockSpec((None, bq, hw), lambda b, h, i: (b, i, seg(Z_Q) + h)),
            pl.BlockSpec((None, None, length, hw), lambda b, h, i: (layer, b, 0, h)),
            pl.BlockSpec((None, None, length, hw), lambda b, h, i: (layer, b, 0, h)),
            pl.BlockSpec((None, nh, nk, 1, bk), lambda b, h, i: (b, h, 0, 0, 0)),
            pl.BlockSpec((None, bq, hw), lambda b, h, i: (b, i, seg(Z_GATE_A) + h)),
        ],
        out_specs=pl.BlockSpec((None, bq, hw), lambda b, h, i: (b, i, h)),
        out_shape=jax.ShapeDtypeStruct((bsz, length, WIDTH), BF16),
        scratch_shapes=[pltpu.VMEM((nh, length, HEAD_DIM), BF16),
                        pltpu.VMEM((nh, length, 2 * HEAD_DIM), BF16)],
        compiler_params=_params("arbitrary", "arbitrary", "arbitrary"),
        name="fox_prompt",
    )(z3, k_all, v_all, c5, z3)


def _fox_decode_kernel(q_ref, kn_ref, vn_ref, cn_ref, ga_ref, kp_ref, vp_ref, cp_ref, o_ref,
                       m_ref, acc_ref, l_ref, *, n, scale):
    j = pl.program_id(1)
    nrow = q_ref.shape[0]
    q = (q_ref[...] * (scale * LOG2E)).astype(BF16)

    @pl.when(j == 0)
    def _():
        m_ref[...] = jnp.full(m_ref.shape, -jnp.inf, F32)
        acc_ref[...] = jnp.zeros(acc_ref.shape, F32)
        l_ref[...] = jnp.zeros(l_ref.shape, F32)

    def update(kblk, vblk, c_row, visible):
        t = jnp.where(visible, _mm_nt(q, kblk) - c_row * LOG2E, -jnp.inf)
        m = m_ref[...]
        m_new = jnp.maximum(m, jnp.broadcast_to(jnp.max(t, axis=1, keepdims=True), m.shape))
        alpha = jnp.exp2(m - m_new)
        p = jnp.concatenate([jnp.exp2(t[:, c * LANES:(c + 1) * LANES] - m_new)
                             for c in range(t.shape[1] // LANES)], axis=1)
        m_ref[...] = m_new
        l_ref[...] = alpha * l_ref[...] + jnp.broadcast_to(jnp.sum(p, axis=1, keepdims=True), m.shape)
        acc_ref[...] = alpha * acc_ref[...] + _mm(p, vblk)

    cols = kp_ref.shape[0]
    row_head = lax.broadcasted_iota(jnp.int32, (nrow, cols), 0) // n
    col_head = lax.broadcasted_iota(jnp.int32, (nrow, cols), 1) % HEADS
    update(kp_ref[...], vp_ref[...], cp_ref[...], row_head == col_head)

    @pl.when(j == pl.num_programs(1) - 1)
    def _():
        r = lax.broadcasted_iota(jnp.int32, (nrow, nrow), 0)
        c = lax.broadcasted_iota(jnp.int32, (nrow, nrow), 1)
        update(kn_ref[...], vn_ref[...], cn_ref[...], (r // n == c // n) & (c % n <= r % n))
        o_ref[...] = (acc_ref[...] / l_ref[...] * _silu(ga_ref[...])).astype(BF16)


def _fox_decode(z3, k4, v4, n_tok, cache_k, cache_v, layer, cp_flat, cn_rows):
    bsz = z3.shape[0]
    past = cache_k.shape[2]
    nrow = HEADS * n_tok
    pc = min(1024, past)
    assert past % pc == 0 and nrow % LANES == 0

    def head_rows(a):
        return a.reshape(bsz, n_tok, HEADS, HEAD_DIM).transpose(0, 2, 1, 3).reshape(bsz, nrow, HEAD_DIM)

    seg = lambda s: z3[:, :n_tok, s * WIDTH:(s + 1) * WIDTH]
    k2 = cache_k.reshape(cache_k.shape[:2] + (past * HEADS, HEAD_DIM))
    v2 = cache_v.reshape(cache_v.shape[:2] + (past * HEADS, HEAD_DIM))
    rows_blk = pl.BlockSpec((None, nrow, HEAD_DIM), lambda b, j: (b, 0, 0))
    cache_blk = pl.BlockSpec((None, None, pc * HEADS, HEAD_DIM), lambda b, j: (layer, b, j, 0))
    o_rows = pl.pallas_call(
        functools.partial(_fox_decode_kernel, n=n_tok, scale=HEAD_DIM ** -0.5),
        grid=(bsz, past // pc),
        in_specs=[
            rows_blk, rows_blk, rows_blk,
            pl.BlockSpec((None, 1, nrow), lambda b, j: (b, 0, 0)),
            rows_blk, cache_blk, cache_blk,
            pl.BlockSpec((None, None, 1, pc * HEADS), lambda b, j: (layer, b, 0, j)),
        ],
        out_specs=rows_blk,
        out_shape=jax.ShapeDtypeStruct((bsz, nrow, HEAD_DIM), BF16),
        scratch_shapes=[pltpu.VMEM((nrow, LANES), F32), pltpu.VMEM((nrow, HEAD_DIM), F32),
                        pltpu.VMEM((nrow, LANES), F32)],
        compiler_params=_params("arbitrary", "arbitrary"),
        name="fox_decode",
    )(head_rows(seg(Z_Q)), head_rows(k4[layer, :, :n_tok]), head_rows(v4[layer, :, :n_tok]),
      cn_rows.reshape(bsz, 1, nrow), head_rows(seg(Z_GATE_A)), k2, v2, cp_flat)
    return o_rows.reshape(bsz, HEADS, n_tok, HEAD_DIM).transpose(0, 2, 1, 3).reshape(bsz, n_tok, WIDTH)


def _gdn_kernel(qp_ref, kp_ref, vp_ref, cw_ref, c0_ref, cols_ref, gr_ref, gb_ref, nw_ref, s0_ref,
                og_ref, s_ref, xp_ref, *, chunk, scale):
    pad = SUBLANES
    nb = qp_ref.shape[0]

    @pl.when(pl.program_id(1) == 0)
    def _():
        s_ref[...] = s0_ref[...]
        xp_ref[:, 0:pad, :] = c0_ref[...]

    convs = []
    for bb in range(nb):
        xp_ref[bb, pad:pad + chunk, 0:WIDTH] = qp_ref[bb]
        xp_ref[bb, pad:pad + chunk, WIDTH:2 * WIDTH] = kp_ref[bb]
        xp_ref[bb, pad:pad + chunk, 2 * WIDTH:3 * WIDTH] = vp_ref[bb]
        xp = xp_ref[bb]
        acc = xp * cw_ref[0:1, :]
        for i in range(1, CONV_W):
            acc = pltpu.roll(acc, 1, axis=0) + xp * cw_ref[i:i + 1, :]
        convs.append(_silu(acc[pad:pad + chunk]))
        xp_ref[bb, 0:pad, :] = xp_ref[bb, chunk:chunk + pad, :]

    row = lax.broadcasted_iota(jnp.int32, (chunk, chunk), 0)
    col = lax.broadcasted_iota(jnp.int32, (chunk, chunk), 1)
    incl = row >= col
    strict = row > col
    sizes = [INV_BASE << i for i in range(int(math.log2(chunk // INV_BASE)) + 1)]
    same = [(row >> int(math.log2(b))) == (col >> int(math.log2(b))) for b in sizes[:-1]]
    diag_mask = strict & same[0]
    off_masks = [strict & ~same[i] & (same[i + 1] if i + 1 < len(same) else True)
                 for i in range(len(same))]

    units = [(bb, hh) for bb in range(nb) for hh in range(HEADS)]
    heads = range(len(units))
    sl = lambda seg, hh: slice(seg * WIDTH + hh * HEAD_DIM, seg * WIDTH + (hh + 1) * HEAD_DIM)
    q, k, v, gc, beta, e, lmat, qk = ([None] * len(units) for _ in range(8))
    for h, (bb, hh) in enumerate(units):
        qh, kh = convs[bb][:, sl(0, hh)], convs[bb][:, sl(1, hh)]
        q[h] = qh * (lax.rsqrt(jnp.sum(qh * qh, axis=1, keepdims=True) + RMS_EPS) * scale)
        k[h] = kh * lax.rsqrt(jnp.sum(kh * kh, axis=1, keepdims=True) + RMS_EPS)
        v[h] = convs[bb][:, sl(2, hh)]
        gc[h] = cols_ref[bb, :, hh:hh + 1]
        beta[h] = cols_ref[bb, :, HEADS + hh:HEADS + hh + 1]
        e[h] = jnp.exp(gc[h])
    for h, (bb, hh) in enumerate(units):
        decay = jnp.exp(jnp.where(incl, gc[h] - gr_ref[bb, hh:hh + 1, :], -jnp.inf))
        kq = _mm_nt(jnp.concatenate([k[h] * beta[h], q[h]], axis=0), k[h])
        lmat[h] = jnp.where(strict, kq[:chunk] * decay, 0.0)
        qk[h] = jnp.where(incl, kq[chunk:] * decay, 0.0)
    xpow = [jnp.where(diag_mask, lmat[h], 0.0) for h in heads]
    nm = [-xpow[h] for h in heads]
    for _ in range(int(math.log2(INV_BASE)) - 1):
        xpow = [_mm(xpow[h], xpow[h]) for h in heads]
        nm = [nm[h] + xpow[h] + _mm(nm[h], xpow[h]) for h in heads]
    for off in off_masks:
        loff = [jnp.where(off, lmat[h], 0.0) for h in heads]
        t_loff = [loff[h] + _mm(nm[h], loff[h]) for h in heads]
        nm = [nm[h] - (t_loff[h] + _mm(t_loff[h], nm[h])) for h in heads]
    uw = []
    for h in heads:
        rhs = jnp.concatenate([v[h] * beta[h], k[h] * (beta[h] * e[h])], axis=1)
        uw.append(rhs + _mm(nm[h], rhs))
    s_old = [s_ref[bb, hh] for bb, hh in units]
    ws_qs = [_mm(jnp.concatenate([uw[h][:, HEAD_DIM:], q[h] * e[h]], axis=0), s_old[h]) for h in heads]
    v_new = [uw[h][:, :HEAD_DIM] - ws_qs[h][:chunk] for h in heads]
    for h, (bb, hh) in enumerate(units):
        g_last = gc[h][chunk - 1:chunk, :]
        k_tail = k[h] * jnp.exp(g_last - gc[h])
        s_ref[bb, hh] = s_old[h] * jnp.exp(g_last) + _mm_tn(k_tail, v_new[h])
    for h, (bb, hh) in enumerate(units):
        o = ws_qs[h][chunk:] + _mm(qk[h], v_new[h])
        o = o * lax.rsqrt(jnp.mean(o * o, axis=1, keepdims=True) + RMS_EPS) * nw_ref[...]
        og_ref[bb, :, sl(0, hh)] = (o * _silu(gb_ref[bb, :, sl(0, hh)])).astype(BF16)


def _gdn(z3, conv_w, conv0, cols, gc_rows, norm_w, s0):
    bsz, length, _ = z3.shape
    chunk = GDN_CHUNK
    assert length % chunk == 0
    nc = length // chunk
    nb = GDN_BATCHES if bsz % GDN_BATCHES == 0 else 1
    gr = gc_rows.reshape(bsz, HEADS, nc, chunk).transpose(0, 2, 1, 3)
    c0 = jnp.pad(conv0, ((0, 0), (SUBLANES - (CONV_W - 1), 0), (0, 0)))
    zblk = lambda seg: pl.BlockSpec((nb, chunk, WIDTH), lambda b, l: (b, l, seg))
    return pl.pallas_call(
        functools.partial(_gdn_kernel, chunk=chunk, scale=HEAD_DIM ** -0.5),
        grid=(bsz // nb, nc),
        in_specs=[
            zblk(Z_QKV_B), zblk(Z_QKV_B + 1), zblk(Z_QKV_B + 2),
            pl.BlockSpec((CONV_W, CONV_CH), lambda b, l: (0, 0)),
            pl.BlockSpec((nb, SUBLANES, CONV_CH), lambda b, l: (b, 0, 0)),
            pl.BlockSpec((nb, chunk, LANES), lambda b, l: (b, l, 0)),
            pl.BlockSpec((nb, None, HEADS, chunk), lambda b, l: (b, l, 0, 0)),
            zblk(Z_GATE_B),
            pl.BlockSpec((1, HEAD_DIM), lambda b, l: (0, 0)),
            pl.BlockSpec((nb, HEADS, HEAD_DIM, HEAD_DIM), lambda b, l: (b, 0, 0, 0)),
        ],
        out_specs=[
            pl.BlockSpec((nb, chunk, WIDTH), lambda b, l: (b, l, 0)),
            pl.BlockSpec((nb, HEADS, HEAD_DIM, HEAD_DIM), lambda b, l: (b, 0, 0, 0)),
        ],
        out_shape=[
            jax.ShapeDtypeStruct((bsz, length, WIDTH), BF16),
            jax.ShapeDtypeStruct((bsz, HEADS, HEAD_DIM, HEAD_DIM), F32),
        ],
        scratch_shapes=[pltpu.VMEM((nb, chunk + SUBLANES, CONV_CH), F32)],
        compiler_params=_params("arbitrary", "arbitrary"),
        name="gdn",
    )(z3, z3, z3, conv_w, c0, cols, gr, z3, norm_w.reshape(1, HEAD_DIM), s0)


def _out_kernel(oa_ref, ob_ref, ma_ref, mb_ref, x_ref, p_ref, wfox_ref, wgdn_ref, wout_ref,
                wplp_ref, wplg_ref, vec_ref, o_ref, *, alpha):
    ya = jnp.dot(oa_ref[...], wfox_ref[...], preferred_element_type=F32)
    yb = jnp.dot(ob_ref[...], wgdn_ref[...], preferred_element_type=F32)
    merged = _sigmoid(ma_ref[...]) * ya + _sigmoid(mb_ref[...]) * yb
    y = alpha * x_ref[...] + _mm(merged, wout_ref[...])
    yc = y - jnp.mean(y, axis=1, keepdims=True)
    var = jnp.mean(yc * yc, axis=1, keepdims=True)
    x1 = yc * lax.rsqrt(var + LN_EPS) * vec_ref[0:1, :] + vec_ref[1:2, :]
    e = _mm(p_ref[...], wplp_ref[...])
    e = e * lax.rsqrt(jnp.mean(e * e, axis=1, keepdims=True) + RMS_EPS) * vec_ref[2:3, :]
    o_ref[...] = x1 + _sigmoid(_mm(x1, wplg_ref[...])) * e


def _out_block(oa, ob, z2d, x2d, p_all, layer, wfox, wgdn, wout, wplp, wplg, vec, alpha):
    t_rows, d = x2d.shape
    ple = p_all.shape[-1]
    tm = min(256, t_rows)
    dpw = d // WIDTH
    assert d % WIDTH == 0 and Z_M_A % dpw == 0 and Z_M_B % dpw == 0
    const = lambda shape: pl.BlockSpec((None,) + shape, lambda r: (layer,) + (0,) * len(shape),
                                       pipeline_mode=pl.Buffered(1))
    return pl.pallas_call(
        functools.partial(_out_kernel, alpha=alpha),
        grid=(t_rows // tm,),
        in_specs=[
            pl.BlockSpec((tm, WIDTH), lambda r: (r, 0)),
            pl.BlockSpec((tm, WIDTH), lambda r: (r, 0)),
            pl.BlockSpec((tm, d), lambda r: (r, Z_M_A // dpw)),
            pl.BlockSpec((tm, d), lambda r: (r, Z_M_B // dpw)),
            pl.BlockSpec((tm, d), lambda r: (r, 0)),
            pl.BlockSpec((None, tm, ple), lambda r: (layer, r, 0)),
            const((WIDTH, d)), const((WIDTH, d)), const((d, d)), const((ple, d)), const((d, d)),
            const((SUBLANES, d)),
        ],
        out_specs=pl.BlockSpec((tm, d), lambda r: (r, 0)),
        out_shape=jax.ShapeDtypeStruct((t_rows, d), F32),
        compiler_params=_params("arbitrary"),
        name="out_block",
    )(oa, ob, z2d, z2d, x2d, p_all, wfox, wgdn, wout, wplp, wplg, vec)


def _layer(x2d, bsz, n_tok, p_all, layer, wts, conv0, s0, alpha, k_all, v_all, fox_cache=None):
    lp = x2d.shape[0] // bsz
    z2d, small, k_all, v_all = _proj(x2d, wts["w_big"], wts["w_small"], wts["par"], k_all, v_all, layer)
    z3 = z2d.reshape(bsz, lp, -1)
    k4 = k_all.reshape(-1, bsz, lp, WIDTH)
    v4 = v_all.reshape(-1, bsz, lp, WIDTH)
    small3 = small.reshape(bsz, lp, LANES)
    logf_all = small3[:, :, 0:HEADS]
    logf = logf_all[:, :n_tok]
    valid = (jnp.arange(lp) < n_tok)[None, :, None]
    g = jnp.where(valid, small3[:, :, HEADS:2 * HEADS], 0.0)
    beta = jnp.where(valid, small3[:, :, 2 * HEADS:3 * HEADS], 0.0)

    to_rows = lambda a: a.transpose(0, 2, 1).reshape(bsz * HEADS, a.shape[1])
    c_rows = _cumsum_rows(to_rows(logf_all), init=None if fox_cache is None else fox_cache[3][layer])
    gc_rows = _cumsum_rows(to_rows(g), period=GDN_CHUNK)
    gc = gc_rows.reshape(bsz, HEADS, lp).transpose(0, 2, 1)
    cols = jnp.concatenate([gc, beta, jnp.zeros((bsz, lp, LANES - 2 * HEADS), F32)], axis=-1)

    if fox_cache is None:
        oa = _fox_prompt(z3, k4, v4, layer, c_rows)
    else:
        cache_k, cache_v, cp_flat, _ = fox_cache
        oa = _fox_decode(z3, k4, v4, n_tok, cache_k, cache_v, layer, cp_flat, c_rows[:, :n_tok])
        oa = jnp.pad(oa, ((0, 0), (0, lp - n_tok), (0, 0)))
    ob, s_new = _gdn(z3, wts["conv_w"], conv0, cols, gc_rows, wts["norm_w"], s0)

    x_out = _out_block(oa.reshape(bsz * lp, WIDTH), ob.reshape(bsz * lp, WIDTH), z2d, x2d, p_all, layer,
                       wts["wfox"], wts["wgdn"], wts["wout"], wts["wplp"], wts["wplg"], wts["vec"], alpha)

    new_conv = z3[:, n_tok - (CONV_W - 1):n_tok, Z_QKV_B * WIDTH:Z_QKV_B * WIDTH + CONV_CH]
    return x_out, k_all, v_all, logf, new_conv, s_new


def kernel(x_prompt, x_sample, cache_fox_k, cache_fox_v, cache_fox_logf, state_gdn_conv, state_gdn, p_prompt, p_sample, w_in, fox_f_bias, gdn_conv_w, gdn_a_log, gdn_dt_bias, gdn_norm_w, w_out_fox, w_out_gdn, w_out, ln_g, ln_b, w_pl_proj, w_pl_gate, pl_norm_w):
    depth = w_in.shape[0]
    bp, seq, d = x_prompt.shape
    bs, dseq, _ = x_sample.shape
    past = cache_fox_k.shape[2]
    ple = p_prompt.shape[-1]
    assert d % WIDTH == 0 and dseq >= CONV_W - 1 and seq >= CONV_W - 1
    alpha = (2 * depth) ** 0.25

    o_f = 3 * WIDTH
    o_ga = o_f + HEADS
    o_qkv = o_ga + WIDTH
    o_a = o_qkv + CONV_CH
    o_gb = o_a + 2 * HEADS
    w_in_b = w_in.astype(BF16)
    w_big = jnp.concatenate([w_in_b[..., :WIDTH], w_in_b[..., o_ga:o_a], w_in_b[..., o_gb:],
                             w_in_b[..., WIDTH:o_f]], axis=-1)
    w_small = jnp.concatenate([w_in[..., o_f:o_ga], w_in[..., o_a:o_gb],
                               jnp.zeros((depth, d, LANES - 3 * HEADS), F32)], axis=-1)
    w_small_hi = w_small.astype(BF16)
    w_small = jnp.stack([w_small_hi, (w_small - w_small_hi.astype(F32)).astype(BF16)], axis=1)
    zpad = jnp.zeros((depth, LANES - 2 * HEADS), F32)
    par = jnp.stack([jnp.concatenate([fox_f_bias, gdn_dt_bias, zpad], axis=-1),
                     jnp.concatenate([jnp.zeros_like(fox_f_bias), gdn_a_log, zpad], axis=-1)], axis=1)
    par = jnp.pad(par, ((0, 0), (0, SUBLANES - 2), (0, 0)))
    vec = jnp.pad(jnp.stack([ln_g, ln_b, pl_norm_w], axis=1), ((0, 0), (0, SUBLANES - 3), (0, 0)))
    wfox, wgdn, wout = w_out_fox.astype(BF16), w_out_gdn.astype(BF16), w_out.astype(BF16)
    wplp, wplg = w_pl_proj.astype(BF16), w_pl_gate.astype(BF16)

    def layer_weights(i):
        return dict(w_big=w_big, w_small=w_small, par=par, conv_w=gdn_conv_w[i],
                    norm_w=gdn_norm_w[i], wfox=wfox, wgdn=wgdn, wout=wout,
                    wplp=wplp, wplg=wplg, vec=vec)

    lp_s = -(-dseq // GDN_CHUNK) * GDN_CHUNK
    xs = jnp.pad(x_sample, ((0, 0), (0, lp_s - dseq), (0, 0))).reshape(bs * lp_s, d)
    ps_all = jnp.pad(p_sample, ((0, 0), (0, 0), (0, lp_s - dseq), (0, 0))).reshape(depth, bs * lp_s, ple)
    xp = x_prompt.reshape(bp * seq, d)
    pp_all = p_prompt.reshape(depth, bp * seq, ple)

    cp_rows = _cumsum_rows(cache_fox_logf.transpose(0, 1, 3, 2).reshape(depth * bs * HEADS, past))
    cp_rows = cp_rows.reshape(depth, bs, HEADS, past)
    cp_flat = cp_rows.transpose(0, 1, 3, 2).reshape(depth, bs, 1, past * HEADS)
    cp_last = cp_rows[..., past - 1:].reshape(depth, bs * HEADS, 1)

    conv0_p = jnp.zeros((bp, CONV_W - 1, CONV_CH), F32)
    s0_p = jnp.zeros((bp, HEADS, HEAD_DIM, HEAD_DIM), F32)
    kp = lax.empty((depth, bp * seq, WIDTH), F32)
    vp = lax.empty((depth, bp * seq, WIDTH), F32)
    ks = lax.empty((depth, bs * lp_s, WIDTH), F32)
    vs = lax.empty((depth, bs * lp_s, WIDTH), F32)
    outs_p, outs_s = [], []
    for i in range(depth):
        wts = layer_weights(i)
        xp, kp, vp, *rest_p = _layer(xp, bp, seq, pp_all, i, wts, conv0_p, s0_p, alpha, kp, vp)
        xs, ks, vs, *rest_s = _layer(xs, bs, dseq, ps_all, i, wts, state_gdn_conv[i], state_gdn[i], alpha,
                                     ks, vs, fox_cache=(cache_fox_k, cache_fox_v, cp_flat, cp_last))
        outs_p.append(rest_p)
        outs_s.append(rest_s)
    stack = lambda outs, j: jnp.stack([o[j] for o in outs])
    heads_p = lambda a: a.reshape(depth, bp, seq, HEADS, HEAD_DIM)
    heads_s = lambda a: a.reshape(depth, bs, lp_s, HEADS, HEAD_DIM)[:, :, :dseq]
    y_prompt = xp.reshape(bp, seq, d)
    y_sample = xs.reshape(bs, lp_s, d)[:, :dseq]
    return (y_prompt, y_sample,
            heads_p(kp), heads_p(vp), stack(outs_p, 0), stack(outs_p, 1), stack(outs_p, 2),
            heads_s(ks), heads_s(vs), stack(outs_s, 0), stack(outs_s, 1), stack(outs_s, 2))
```

```python
import functools
import math

import jax
import jax.numpy as jnp
from jax import lax
from jax.experimental import pallas as pl
from jax.experimental.pallas import tpu as pltpu

F32 = jnp.float32
BF16 = jnp.bfloat16

HEADS = 8
HEAD_DIM = 128
WIDTH = HEADS * HEAD_DIM
CONV_W = 4
CONV_CH = 3 * WIDTH
LN_EPS = 1e-5
RMS_EPS = 1e-6
LOG2E = math.log2(math.e)
LANES = 128
SUBLANES = 8
GDN_CHUNK = 128
GDN_BATCHES = 2
INV_BASE = 8
FOX_BQ = 1024
FOX_BK = 1024
FOX_ROWS = 128
FOX_HEADS_PER_STEP = 2
FOX_STRIP = 16
VMEM_LIMIT = 56 * 1024 * 1024

Z_Q, Z_GATE_A, Z_QKV_B, Z_GATE_B, Z_M_A, Z_M_B, Z_END = 0, 1, 2, 5, 6, 8, 10


def _params(*sem):
    return pltpu.CompilerParams(dimension_semantics=sem, vmem_limit_bytes=VMEM_LIMIT)


def _mm(a, b):
    return jnp.dot(a.astype(BF16), b.astype(BF16), preferred_element_type=F32)


def _mm_nt(a, b):
    return lax.dot_general(a.astype(BF16), b.astype(BF16), (((1,), (1,)), ((), ())),
                           preferred_element_type=F32)


def _mm_tn(a, b):
    return lax.dot_general(a.astype(BF16), b.astype(BF16), (((0,), (0,)), ((), ())),
                           preferred_element_type=F32)


def _sigmoid(x):
    return 0.5 * jnp.tanh(0.5 * x) + 0.5


def _silu(x):
    return x * _sigmoid(x)


def _proj_kernel(x_ref, w_ref, ws_ref, par_ref, k_in_ref, v_in_ref, z_ref, s_ref, k_ref, v_ref, xb_ref,
                 *, z_tiles, kv_tiles):
    del k_in_ref, v_in_ref
    j = pl.program_id(1)

    @pl.when(j == 0)
    def _():
        x = x_ref[...]
        xb = x.astype(BF16)
        xb_ref[...] = xb
        x_lo = (x - xb.astype(F32)).astype(BF16)
        raw = (jnp.dot(xb, ws_ref[0], preferred_element_type=F32)
               + jnp.dot(x_lo, ws_ref[0], preferred_element_type=F32)
               + jnp.dot(xb, ws_ref[1], preferred_element_type=F32))
        y = raw + par_ref[0:1, :]
        neg_a = -jnp.exp(par_ref[1:2, :])
        t = jnp.log1p(jnp.exp(-jnp.abs(y)))
        logf = jnp.minimum(y, 0.0) - t
        g = neg_a * (jnp.maximum(y, 0.0) + t)
        beta = _sigmoid(raw)
        lane = lax.broadcasted_iota(jnp.int32, raw.shape, 1)
        s_ref[...] = jnp.where(lane < HEADS, logf,
                               jnp.where(lane < 2 * HEADS, g,
                                         jnp.where(lane < 3 * HEADS, beta, 0.0)))

    @pl.when(j < z_tiles)
    def _():
        z_ref[...] = _mm_nt(xb_ref[...], w_ref[...])

    @pl.when((j >= z_tiles) & (j < z_tiles + kv_tiles))
    def _():
        k_ref[...] = _mm_nt(xb_ref[...], w_ref[...])

    @pl.when(j >= z_tiles + kv_tiles)
    def _():
        v_ref[...] = _mm_nt(xb_ref[...], w_ref[...])


def _proj(x2d, w_big, w_small, par, k_all, v_all, layer):
    t_rows, d = x2d.shape
    tm = min(1024, t_rows)
    tn = 512
    kv_tiles = WIDTH // tn
    z_tiles = w_big.shape[1] // tn - 2 * kv_tiles
    clamp = lambda j, lo: jnp.clip(j - lo, 0, kv_tiles - 1)
    return pl.pallas_call(
        functools.partial(_proj_kernel, z_tiles=z_tiles, kv_tiles=kv_tiles),
        grid=(t_rows // tm, z_tiles + 2 * kv_tiles),
        in_specs=[
            pl.BlockSpec((tm, d), lambda i, j: (i, 0)),
            pl.BlockSpec((None, tn, d), lambda i, j: (layer, j, 0)),
            pl.BlockSpec((None, 2, d, LANES), lambda i, j: (layer, 0, 0, 0)),
            pl.BlockSpec((None, SUBLANES, LANES), lambda i, j: (layer, 0, 0)),
            pl.BlockSpec(memory_space=pl.ANY),
            pl.BlockSpec(memory_space=pl.ANY),
        ],
        out_specs=[
            pl.BlockSpec((tm, tn), lambda i, j: (i, jnp.minimum(j, z_tiles - 1))),
            pl.BlockSpec((tm, LANES), lambda i, j: (i, 0)),
            pl.BlockSpec((None, tm, tn), lambda i, j: (layer, i, clamp(j, z_tiles))),
            pl.BlockSpec((None, tm, tn), lambda i, j: (layer, i, clamp(j, z_tiles + kv_tiles))),
        ],
        out_shape=[
            jax.ShapeDtypeStruct((t_rows, z_tiles * tn), F32),
            jax.ShapeDtypeStruct((t_rows, LANES), F32),
            jax.ShapeDtypeStruct(k_all.shape, F32),
            jax.ShapeDtypeStruct(v_all.shape, F32),
        ],
        input_output_aliases={4: 2, 5: 3},
        scratch_shapes=[pltpu.VMEM((tm, d), BF16)],
        compiler_params=_params("arbitrary", "arbitrary"),
        name="proj",
    )(x2d, w_big, w_small, par, k_all, v_all)


def _cumsum_kernel(x_ref, init_ref, o_ref, carry_ref, *, blk, period):
    r = lax.broadcasted_iota(jnp.int32, (blk, blk), 0)
    c = lax.broadcasted_iota(jnp.int32, (blk, blk), 1)
    keep = r <= c
    if period is not None:
        keep = keep & ((r // period) == (c // period))
    y = jnp.dot(x_ref[...], keep.astype(F32), precision=lax.Precision.HIGHEST,
                preferred_element_type=F32)
    if period is None:
        @pl.when(pl.program_id(0) == 0)
        def _():
            carry_ref[...] = init_ref[...]

        y = y + carry_ref[...]
        carry_ref[...] = y[:, blk - 1:blk]
    o_ref[...] = y


def _cumsum_rows(x, period=None, init=None):
    rows, length = x.shape
    blk = min(512, length)
    assert length % blk == 0 and (period is None or (blk % period == 0 and init is None))
    if init is None:
        init = jnp.zeros((rows, 1), F32)
    return pl.pallas_call(
        functools.partial(_cumsum_kernel, blk=blk, period=period),
        grid=(length // blk,),
        in_specs=[pl.BlockSpec((rows, blk), lambda j: (0, j)), pl.BlockSpec((rows, 1), lambda j: (0, 0))],
        out_specs=pl.BlockSpec((rows, blk), lambda j: (0, j)),
        out_shape=jax.ShapeDtypeStruct((rows, length), F32),
        scratch_shapes=[pltpu.VMEM((rows, 1), F32)],
        compiler_params=_params("arbitrary"),
        name="cumsum",
    )(x, init)


def _gates_kernel(s_ref, init_ref, c_ref, gc_ref, cols_ref, carry_ref, *, blk, period, n_tok):
    j = pl.program_id(1)

    @pl.when(j == 0)
    def _():
        carry_ref[...] = init_ref[...]

    x = s_ref[...]
    pos = j * blk + lax.broadcasted_iota(jnp.int32, x.shape, 0)
    lane = lax.broadcasted_iota(jnp.int32, x.shape, 1)
    x = jnp.where((pos >= n_tok) & (lane >= HEADS), 0.0, x)
    xt = x.T
    r = lax.broadcasted_iota(jnp.int32, (blk, blk), 0)
    c = lax.broadcasted_iota(jnp.int32, (blk, blk), 1)
    upper = r <= c
    local = upper & ((r // period) == (c // period))
    rows = xt[0:2 * HEADS]
    run = jnp.dot(rows, upper.astype(F32), precision=lax.Precision.HIGHEST, preferred_element_type=F32)
    loc = jnp.dot(rows, local.astype(F32), precision=lax.Precision.HIGHEST, preferred_element_type=F32)
    csum = run[0:HEADS] + carry_ref[...]
    carry_ref[...] = csum[:, blk - 1:blk]
    gc = loc[HEADS:2 * HEADS]
    c_ref[...] = csum
    gc_ref[...] = gc
    cols_t = jnp.concatenate([gc, xt[2 * HEADS:3 * HEADS], jnp.zeros((LANES - 2 * HEADS, blk), F32)], axis=0)
    cols_ref[...] = cols_t.T


def _gates(small3, n_tok, init):
    bsz, lp, _ = small3.shape
    blk = min(512, lp)
    assert lp % blk == 0 and blk % GDN_CHUNK == 0
    rows_blk = pl.BlockSpec((None, HEADS, blk), lambda b, j: (b, 0, j))
    return pl.pallas_call(
        functools.partial(_gates_kernel, blk=blk, period=GDN_CHUNK, n_tok=n_tok),
        grid=(bsz, lp // blk),
        in_specs=[pl.BlockSpec((None, blk, LANES), lambda b, j: (b, j, 0)),
                  pl.BlockSpec((None, HEADS, 1), lambda b, j: (b, 0, 0))],
        out_specs=[rows_blk, rows_blk, pl.BlockSpec((None, blk, LANES), lambda b, j: (b, j, 0))],
        out_shape=[jax.ShapeDtypeStruct((bsz, HEADS, lp), F32), jax.ShapeDtypeStruct((bsz, HEADS, lp), F32),
                   jax.ShapeDtypeStruct((bsz, lp, LANES), F32)],
        scratch_shapes=[pltpu.VMEM((HEADS, 1), F32)],
        compiler_params=_params("arbitrary", "arbitrary"),
        name="gates",
    )(small3, init)


def _fox_kernel(q_ref, k_ref, v_ref, c_ref, ga_ref, o_ref, kb_ref, vb_ref, *, bq, bk, rows, scale):
    qi = pl.program_id(2)
    length = k_ref.shape[0]
    nh = kb_ref.shape[0]
    head = lambda hh: slice(hh * HEAD_DIM, (hh + 1) * HEAD_DIM)
    units = [(hh, r * rows) for hh in range(nh) for r in range(bq // rows)]

    @pl.when(qi == 0)
    def _():
        for hh in range(nh):
            kb_ref[hh] = k_ref[:, head(hh)].astype(BF16)
            vb_ref[hh, :, :HEAD_DIM] = v_ref[:, head(hh)].astype(BF16)
            vb_ref[hh, :, HEAD_DIM:] = jnp.ones((length, LANES), BF16)

    q = (q_ref[...] * (scale * LOG2E)).astype(BF16)
    qs = [q[r0:r0 + rows, head(hh)] for hh, r0 in units]

    def update(kj, carries, diag_offset=None):
        start = pl.multiple_of(kj * bk, bk)
        widths = {u: bk if diag_offset is None else min(bk, r0 + rows - diag_offset)
                  for u, (_, r0) in enumerate(units)}
        widths = {u: w for u, w in widths.items() if w > 0}
        keyed = sorted({(units[u][0], w) for u, w in widths.items()})
        kblks = {(hh, w): kb_ref[hh, pl.ds(start, w), :] for hh, w in keyed}
        cks = {(hh, w): c_ref[hh, kj, :, 0:w] * LOG2E for hh, w in keyed}
        ss = {u: lax.dot_general(qs[u], kblks[units[u][0], w], (((1,), (1,)), ((), ())),
                                 preferred_element_type=F32)
              for u, w in widths.items()}
        out = list(carries)
        for u, s in ss.items():
            m, acc, l = carries[u]
            hh, r0 = units[u]
            width = widths[u]
            lane_blocks = [slice(c * LANES, (c + 1) * LANES) for c in range(width // LANES)]
            masked = diag_offset is not None and diag_offset + width - 1 > r0
            m_parts, a_parts, p_parts = [], [], []
            for i in range(rows // FOX_STRIP):
                sl = slice(i * FOX_STRIP, (i + 1) * FOX_STRIP)
                t = s[sl] - cks[hh, width]
                if masked:
                    row = lax.broadcasted_iota(jnp.int32, (FOX_STRIP, width), 0)
                    col = lax.broadcasted_iota(jnp.int32, (FOX_STRIP, width), 1)
                    t = jnp.where(col + (diag_offset - r0 - i * FOX_STRIP) <= row, t, -jnp.inf)
                mx = t[:, lane_blocks[0]]
                for c in lane_blocks[1:]:
                    mx = jnp.maximum(mx, t[:, c])
                m_new = jnp.maximum(m[sl], jnp.broadcast_to(jnp.max(mx, axis=1, keepdims=True), mx.shape))
                m_parts.append(m_new)
                a_parts.append(jnp.exp2(m[sl] - m_new))
                p_parts.append(jnp.concatenate([jnp.exp2(t[:, c] - m_new).astype(BF16) for c in lane_blocks],
                                               axis=1))
            alpha = jnp.concatenate(a_parts, axis=0)
            pv = jnp.dot(jnp.concatenate(p_parts, axis=0), vb_ref[hh, pl.ds(start, width), :],
                         preferred_element_type=F32)
            out[u] = (jnp.concatenate(m_parts, axis=0), alpha * acc + pv[:, :HEAD_DIM],
                      alpha * l + pv[:, HEAD_DIM:])
        return tuple(out)

    init = (jnp.full((rows, LANES), -jnp.inf, F32), jnp.zeros((rows, HEAD_DIM), F32),
            jnp.zeros((rows, LANES), F32))
    per_q = bq // bk
    def past_q_block(j, cr):
        for d in range(per_q):
            cr = update(j * per_q + d, cr)
        return cr

    carries = lax.fori_loop(0, qi, past_q_block, (init,) * len(units))
    for d in range(per_q):
        carries = update(qi * per_q + d, carries, diag_offset=d * bk)
    for (hh, r0), (_, acc, l) in zip(units, carries):
        rs = slice(r0, r0 + rows)
        o_ref[rs, head(hh)] = (acc / l * _silu(ga_ref[rs, head(hh)])).astype(BF16)


def _fox_prompt(z3, k_all, v_all, layer, c_rows):
    bsz, length, _ = z3.shape
    bq = min(FOX_BQ, length)
    bk = min(FOX_BK, bq)
    rows = min(FOX_ROWS, bq)
    assert length % bq == 0 and bq % bk == 0 and bq % rows == 0
    nk = length // bk
    c5 = c_rows.reshape(bsz, HEADS, nk, 1, bk)
    nh = FOX_HEADS_PER_STEP
    hw = nh * HEAD_DIM
    assert HEADS % nh == 0 and WIDTH % hw == 0
    seg = lambda s: s * (WIDTH // hw)
    return pl.pallas_call(
        functools.partial(_fox_kernel, bq=bq, bk=bk, rows=rows, scale=HEAD_DIM ** -0.5),
        grid=(bsz, HEADS // nh, length // bq),
        in_specs=[
            pl.BlockSpec((None, bq, hw), lambda b, h, i: (b, i, seg(Z_Q) + h)),
            pl.BlockSpec((None, None, length, hw), lambda b, h, i: (layer, b, 0, h)),
            pl.BlockSpec((None, None, length, hw), lambda b, h, i: (layer, b, 0, h)),
            pl.BlockSpec((None, nh, nk, 1, bk), lambda b, h, i: (b, h, 0, 0, 0)),
            pl.BlockSpec((None, bq, hw), lambda b, h, i: (b, i, seg(Z_GATE_A) + h)),
        ],
        out_specs=pl.BlockSpec((None, bq, hw), lambda b, h, i: (b, i, h)),
        out_shape=jax.ShapeDtypeStruct((bsz, length, WIDTH), BF16),
        scratch_shapes=[pltpu.VMEM((nh, length, HEAD_DIM), BF16),
                        pltpu.VMEM((nh, length, 2 * HEAD_DIM), BF16)],
        compiler_params=_params("arbitrary", "arbitrary", "arbitrary"),
        name="fox_prompt",
    )(z3, k_all, v_all, c5, z3)


def _fox_decode_kernel(q_ref, kn_ref, vn_ref, cn_ref, ga_ref, kp_ref, vp_ref, cp_ref, o_ref,
                       m_ref, acc_ref, l_ref, *, n, scale):
    j = pl.program_id(1)
    nrow = q_ref.shape[0]
    q = (q_ref[...] * (scale * LOG2E)).astype(BF16)

    @pl.when(j == 0)
    def _():
        m_ref[...] = jnp.full(m_ref.shape, -jnp.inf, F32)
        acc_ref[...] = jnp.zeros(acc_ref.shape, F32)
        l_ref[...] = jnp.zeros(l_ref.shape, F32)

    def update(kblk, vblk, c_row, visible):
        t = jnp.where(visible, _mm_nt(q, kblk) - c_row * LOG2E, -jnp.inf)
        m = m_ref[...]
        m_new = jnp.maximum(m, jnp.broadcast_to(jnp.max(t, axis=1, keepdims=True), m.shape))
        alpha = jnp.exp2(m - m_new)
        p = jnp.concatenate([jnp.exp2(t[:, c * LANES:(c + 1) * LANES] - m_new)
                             for c in range(t.shape[1] // LANES)], axis=1)
        m_ref[...] = m_new
        l_ref[...] = alpha * l_ref[...] + jnp.broadcast_to(jnp.sum(p, axis=1, keepdims=True), m.shape)
        acc_ref[...] = alpha * acc_ref[...] + _mm(p, vblk)

    cols = kp_ref.shape[0]
    row_head = lax.broadcasted_iota(jnp.int32, (nrow, cols), 0) // n
    col_head = lax.broadcasted_iota(jnp.int32, (nrow, cols), 1) % HEADS
    update(kp_ref[...], vp_ref[...], cp_ref[...], row_head == col_head)

    @pl.when(j == pl.num_programs(1) - 1)
    def _():
        r = lax.broadcasted_iota(jnp.int32, (nrow, nrow), 0)
        c = lax.broadcasted_iota(jnp.int32, (nrow, nrow), 1)
        update(kn_ref[...], vn_ref[...], cn_ref[...], (r // n == c // n) & (c % n <= r % n))
        o_ref[...] = (acc_ref[...] / l_ref[...] * _silu(ga_ref[...])).astype(BF16)


def _fox_decode(z3, k4, v4, n_tok, cache_k, cache_v, layer, cp_flat, cn_rows):
    bsz = z3.shape[0]
    past = cache_k.shape[2]
    nrow = HEADS * n_tok
    pc = min(1024, past)
    assert past % pc == 0 and nrow % LANES == 0

    def head_rows(a):
        return a.reshape(bsz, n_tok, HEADS, HEAD_DIM).transpose(0, 2, 1, 3).reshape(bsz, nrow, HEAD_DIM)

    seg = lambda s: z3[:, :n_tok, s * WIDTH:(s + 1) * WIDTH]
    k2 = cache_k.reshape(cache_k.shape[:2] + (past * HEADS, HEAD_DIM))
    v2 = cache_v.reshape(cache_v.shape[:2] + (past * HEADS, HEAD_DIM))
    rows_blk = pl.BlockSpec((None, nrow, HEAD_DIM), lambda b, j: (b, 0, 0))
    cache_blk = pl.BlockSpec((None, None, pc * HEADS, HEAD_DIM), lambda b, j: (layer, b, j, 0))
    o_rows = pl.pallas_call(
        functools.partial(_fox_decode_kernel, n=n_tok, scale=HEAD_DIM ** -0.5),
        grid=(bsz, past // pc),
        in_specs=[
            rows_blk, rows_blk, rows_blk,
            pl.BlockSpec((None, 1, nrow), lambda b, j: (b, 0, 0)),
            rows_blk, cache_blk, cache_blk,
            pl.BlockSpec((None, None, 1, pc * HEADS), lambda b, j: (layer, b, 0, j)),
        ],
        out_specs=rows_blk,
        out_shape=jax.ShapeDtypeStruct((bsz, nrow, HEAD_DIM), BF16),
        scratch_shapes=[pltpu.VMEM((nrow, LANES), F32), pltpu.VMEM((nrow, HEAD_DIM), F32),
                        pltpu.VMEM((nrow, LANES), F32)],
        compiler_params=_params("arbitrary", "arbitrary"),
        name="fox_decode",
    )(head_rows(seg(Z_Q)), head_rows(k4[layer, :, :n_tok]), head_rows(v4[layer, :, :n_tok]),
      cn_rows.reshape(bsz, 1, nrow), head_rows(seg(Z_GATE_A)), k2, v2, cp_flat)
    return o_rows.reshape(bsz, HEADS, n_tok, HEAD_DIM).transpose(0, 2, 1, 3).reshape(bsz, n_tok, WIDTH)


def _gdn_kernel(qp_ref, kp_ref, vp_ref, cw_ref, c0_ref, cols_ref, gr_ref, gb_ref, nw_ref, s0_ref,
                og_ref, s_ref, xp_ref, *, chunk, scale):
    pad = SUBLANES
    nb = qp_ref.shape[0]

    @pl.when(pl.program_id(1) == 0)
    def _():
        s_ref[...] = s0_ref[...]
        xp_ref[:, 0:pad, :] = c0_ref[...]

    convs = []
    for bb in range(nb):
        xp_ref[bb, pad:pad + chunk, 0:WIDTH] = qp_ref[bb]
        xp_ref[bb, pad:pad + chunk, WIDTH:2 * WIDTH] = kp_ref[bb]
        xp_ref[bb, pad:pad + chunk, 2 * WIDTH:3 * WIDTH] = vp_ref[bb]
        xp = xp_ref[bb]
        acc = xp * cw_ref[0:1, :]
        for i in range(1, CONV_W):
            acc = pltpu.roll(acc, 1, axis=0) + xp * cw_ref[i:i + 1, :]
        convs.append(_silu(acc[pad:pad + chunk]))
        xp_ref[bb, 0:pad, :] = xp_ref[bb, chunk:chunk + pad, :]

    row = lax.broadcasted_iota(jnp.int32, (chunk, chunk), 0)
    col = lax.broadcasted_iota(jnp.int32, (chunk, chunk), 1)
    incl = row >= col
    strict = row > col
    sizes = [INV_BASE << i for i in range(int(math.log2(chunk // INV_BASE)) + 1)]
    same = [(row >> int(math.log2(b))) == (col >> int(math.log2(b))) for b in sizes[:-1]]
    diag_mask = strict & same[0]
    off_masks = [strict & ~same[i] & (same[i + 1] if i + 1 < len(same) else True)
                 for i in range(len(same))]

    units = [(bb, hh) for bb in range(nb) for hh in range(HEADS)]
    heads = range(len(units))
    sl = lambda seg, hh: slice(seg * WIDTH + hh * HEAD_DIM, seg * WIDTH + (hh + 1) * HEAD_DIM)
    q, k, v, gc, beta, e, lmat, qk = ([None] * len(units) for _ in range(8))
    for h, (bb, hh) in enumerate(units):
        qh, kh = convs[bb][:, sl(0, hh)], convs[bb][:, sl(1, hh)]
        q[h] = qh * (lax.rsqrt(jnp.sum(qh * qh, axis=1, keepdims=True) + RMS_EPS) * scale)
        k[h] = kh * lax.rsqrt(jnp.sum(kh * kh, axis=1, keepdims=True) + RMS_EPS)
        v[h] = convs[bb][:, sl(2, hh)]
        gc[h] = cols_ref[bb, :, hh:hh + 1]
        beta[h] = cols_ref[bb, :, HEADS + hh:HEADS + hh + 1]
        e[h] = jnp.exp(gc[h])
    for h, (bb, hh) in enumerate(units):
        decay = jnp.exp(jnp.where(incl, gc[h] - gr_ref[bb, hh:hh + 1, :], -jnp.inf))
        kq = _mm_nt(jnp.concatenate([k[h] * beta[h], q[h]], axis=0), k[h])
        lmat[h] = jnp.where(strict, kq[:chunk] * decay, 0.0)
        qk[h] = jnp.where(incl, kq[chunk:] * decay, 0.0)
    xpow = [jnp.where(diag_mask, lmat[h], 0.0) for h in heads]
    nm = [-xpow[h] for h in heads]
    for _ in range(int(math.log2(INV_BASE)) - 1):
        xpow = [_mm(xpow[h], xpow[h]) for h in heads]
        nm = [nm[h] + xpow[h] + _mm(nm[h], xpow[h]) for h in heads]
    for off in off_masks:
        loff = [jnp.where(off, lmat[h], 0.0) for h in heads]
        t_loff = [loff[h] + _mm(nm[h], loff[h]) for h in heads]
        nm = [nm[h] - (t_loff[h] + _mm(t_loff[h], nm[h])) for h in heads]
    uw = []
    for h in heads:
        rhs = jnp.concatenate([v[h] * beta[h], k[h] * (beta[h] * e[h])], axis=1)
        uw.append(rhs + _mm(nm[h], rhs))
    s_old = [s_ref[bb, hh] for bb, hh in units]
    ws_qs = [_mm(jnp.concatenate([uw[h][:, HEAD_DIM:], q[h] * e[h]], axis=0), s_old[h]) for h in heads]
    v_new = [uw[h][:, :HEAD_DIM] - ws_qs[h][:chunk] for h in heads]
    for h, (bb, hh) in enumerate(units):
        g_last = gc[h][chunk - 1:chunk, :]
        k_tail = k[h] * jnp.exp(g_last - gc[h])
        s_ref[bb, hh] = s_old[h] * jnp.exp(g_last) + _mm_tn(k_tail, v_new[h])
    for h, (bb, hh) in enumerate(units):
        o = ws_qs[h][chunk:] + _mm(qk[h], v_new[h])
        o = o * lax.rsqrt(jnp.mean(o * o, axis=1, keepdims=True) + RMS_EPS) * nw_ref[...]
        og_ref[bb, :, sl(0, hh)] = (o * _silu(gb_ref[bb, :, sl(0, hh)])).astype(BF16)


def _gdn(z3, conv_w, conv0, cols, gc_rows, norm_w, s0):
    bsz, length, _ = z3.shape
    chunk = GDN_CHUNK
    assert length % chunk == 0
    nc = length // chunk
    nb = GDN_BATCHES if bsz % GDN_BATCHES == 0 else 1
    c0 = jnp.pad(conv0, ((0, 0), (SUBLANES - (CONV_W - 1), 0), (0, 0)))
    zblk = lambda seg: pl.BlockSpec((nb, chunk, WIDTH), lambda b, l: (b, l, seg))
    return pl.pallas_call(
        functools.partial(_gdn_kernel, chunk=chunk, scale=HEAD_DIM ** -0.5),
        grid=(bsz // nb, nc),
        in_specs=[
            zblk(Z_QKV_B), zblk(Z_QKV_B + 1), zblk(Z_QKV_B + 2),
            pl.BlockSpec((CONV_W, CONV_CH), lambda b, l: (0, 0)),
            pl.BlockSpec((nb, SUBLANES, CONV_CH), lambda b, l: (b, 0, 0)),
            pl.BlockSpec((nb, chunk, LANES), lambda b, l: (b, l, 0)),
            pl.BlockSpec((nb, HEADS, chunk), lambda b, l: (b, 0, l)),
            zblk(Z_GATE_B),
            pl.BlockSpec((1, HEAD_DIM), lambda b, l: (0, 0)),
            pl.BlockSpec((nb, HEADS, HEAD_DIM, HEAD_DIM), lambda b, l: (b, 0, 0, 0)),
        ],
        out_specs=[
            pl.BlockSpec((nb, chunk, WIDTH), lambda b, l: (b, l, 0)),
            pl.BlockSpec((nb, HEADS, HEAD_DIM, HEAD_DIM), lambda b, l: (b, 0, 0, 0)),
        ],
        out_shape=[
            jax.ShapeDtypeStruct((bsz, length, WIDTH), BF16),
            jax.ShapeDtypeStruct((bsz, HEADS, HEAD_DIM, HEAD_DIM), F32),
        ],
        scratch_shapes=[pltpu.VMEM((nb, chunk + SUBLANES, CONV_CH), F32)],
        compiler_params=_params("arbitrary", "arbitrary"),
        name="gdn",
    )(z3, z3, z3, conv_w, c0, cols, gc_rows, z3, norm_w.reshape(1, HEAD_DIM), s0)


def _out_kernel(oa_ref, ob_ref, ma_ref, mb_ref, x_ref, p_ref, wfox_ref, wgdn_ref, wout_ref,
                wplp_ref, wplg_ref, vec_ref, o_ref, *, alpha):
    ya = jnp.dot(oa_ref[...], wfox_ref[...], preferred_element_type=F32)
    yb = jnp.dot(ob_ref[...], wgdn_ref[...], preferred_element_type=F32)
    merged = _sigmoid(ma_ref[...]) * ya + _sigmoid(mb_ref[...]) * yb
    y = alpha * x_ref[...] + _mm(merged, wout_ref[...])
    yc = y - jnp.mean(y, axis=1, keepdims=True)
    var = jnp.mean(yc * yc, axis=1, keepdims=True)
    x1 = yc * lax.rsqrt(var + LN_EPS) * vec_ref[0:1, :] + vec_ref[1:2, :]
    e = _mm(p_ref[...], wplp_ref[...])
    e = e * lax.rsqrt(jnp.mean(e * e, axis=1, keepdims=True) + RMS_EPS) * vec_ref[2:3, :]
    o_ref[...] = x1 + _sigmoid(_mm(x1, wplg_ref[...])) * e


def _out_block(oa, ob, z2d, x2d, p_all, layer, wfox, wgdn, wout, wplp, wplg, vec, alpha):
    t_rows, d = x2d.shape
    ple = p_all.shape[-1]
    tm = min(256, t_rows)
    dpw = d // WIDTH
    assert d % WIDTH == 0 and Z_M_A % dpw == 0 and Z_M_B % dpw == 0
    const = lambda shape: pl.BlockSpec((None,) + shape, lambda r: (layer,) + (0,) * len(shape),
                                       pipeline_mode=pl.Buffered(1))
    return pl.pallas_call(
        functools.partial(_out_kernel, alpha=alpha),
        grid=(t_rows // tm,),
        in_specs=[
            pl.BlockSpec((tm, WIDTH), lambda r: (r, 0)),
            pl.BlockSpec((tm, WIDTH), lambda r: (r, 0)),
            pl.BlockSpec((tm, d), lambda r: (r, Z_M_A // dpw)),
            pl.BlockSpec((tm, d), lambda r: (r, Z_M_B // dpw)),
            pl.BlockSpec((tm, d), lambda r: (r, 0)),
            pl.BlockSpec((None, tm, ple), lambda r: (layer, r, 0)),
            const((WIDTH, d)), const((WIDTH, d)), const((d, d)), const((ple, d)), const((d, d)),
            const((SUBLANES, d)),
        ],
        out_specs=pl.BlockSpec((tm, d), lambda r: (r, 0)),
        out_shape=jax.ShapeDtypeStruct((t_rows, d), F32),
        compiler_params=_params("arbitrary"),
        name="out_block",
    )(oa, ob, z2d, z2d, x2d, p_all, wfox, wgdn, wout, wplp, wplg, vec)


def _layer(x2d, bsz, n_tok, p_all, layer, wts, conv0, s0, alpha, k_all, v_all, fox_cache=None):
    lp = x2d.shape[0] // bsz
    z2d, small, k_all, v_all = _proj(x2d, wts["w_big"], wts["w_small"], wts["par"], k_all, v_all, layer)
    z3 = z2d.reshape(bsz, lp, -1)
    k4 = k_all.reshape(-1, bsz, lp, WIDTH)
    v4 = v_all.reshape(-1, bsz, lp, WIDTH)
    small3 = small.reshape(bsz, lp, LANES)
    logf = small3[:, :n_tok, 0:HEADS]
    c_init = (jnp.zeros((bsz, HEADS, 1), F32) if fox_cache is None
              else fox_cache[3][layer].reshape(bsz, HEADS, 1))
    c_rows, gc_rows, cols = _gates(small3, n_tok, c_init)

    if fox_cache is None:
        oa = _fox_prompt(z3, k4, v4, layer, c_rows)
    else:
        cache_k, cache_v, cp_flat, _ = fox_cache
        oa = _fox_decode(z3, k4, v4, n_tok, cache_k, cache_v, layer, cp_flat, c_rows[:, :, :n_tok])
        oa = jnp.pad(oa, ((0, 0), (0, lp - n_tok), (0, 0)))
    ob, s_new = _gdn(z3, wts["conv_w"], conv0, cols, gc_rows, wts["norm_w"], s0)

    x_out = _out_block(oa.reshape(bsz * lp, WIDTH), ob.reshape(bsz * lp, WIDTH), z2d, x2d, p_all, layer,
                       wts["wfox"], wts["wgdn"], wts["wout"], wts["wplp"], wts["wplg"], wts["vec"], alpha)

    new_conv = z3[:, n_tok - (CONV_W - 1):n_tok, Z_QKV_B * WIDTH:Z_QKV_B * WIDTH + CONV_CH]
    return x_out, k_all, v_all, logf, new_conv, s_new


def kernel(x_prompt, x_sample, cache_fox_k, cache_fox_v, cache_fox_logf, state_gdn_conv, state_gdn, p_prompt, p_sample, w_in, fox_f_bias, gdn_conv_w, gdn_a_log, gdn_dt_bias, gdn_norm_w, w_out_fox, w_out_gdn, w_out, ln_g, ln_b, w_pl_proj, w_pl_gate, pl_norm_w):
    depth = w_in.shape[0]
    bp, seq, d = x_prompt.shape
    bs, dseq, _ = x_sample.shape
    past = cache_fox_k.shape[2]
    ple = p_prompt.shape[-1]
    assert d % WIDTH == 0 and dseq >= CONV_W - 1 and seq >= CONV_W - 1
    alpha = (2 * depth) ** 0.25

    o_f = 3 * WIDTH
    o_ga = o_f + HEADS
    o_qkv = o_ga + WIDTH
    o_a = o_qkv + CONV_CH
    o_gb = o_a + 2 * HEADS
    w_in_t = jnp.swapaxes(w_in, 1, 2).astype(BF16)
    w_big = jnp.concatenate([w_in_t[:, :WIDTH], w_in_t[:, o_ga:o_a], w_in_t[:, o_gb:],
                             w_in_t[:, WIDTH:o_f]], axis=1)
    w_small = jnp.concatenate([w_in[..., o_f:o_ga], w_in[..., o_a:o_gb],
                               jnp.zeros((depth, d, LANES - 3 * HEADS), F32)], axis=-1)
    w_small_hi = w_small.astype(BF16)
    w_small = jnp.stack([w_small_hi, (w_small - w_small_hi.astype(F32)).astype(BF16)], axis=1)
    zpad = jnp.zeros((depth, LANES - 2 * HEADS), F32)
    par = jnp.stack([jnp.concatenate([fox_f_bias, gdn_dt_bias, zpad], axis=-1),
                     jnp.concatenate([jnp.zeros_like(fox_f_bias), gdn_a_log, zpad], axis=-1)], axis=1)
    par = jnp.pad(par, ((0, 0), (0, SUBLANES - 2), (0, 0)))
    vec = jnp.pad(jnp.stack([ln_g, ln_b, pl_norm_w], axis=1), ((0, 0), (0, SUBLANES - 3), (0, 0)))
    wfox, wgdn, wout = w_out_fox.astype(BF16), w_out_gdn.astype(BF16), w_out.astype(BF16)
    wplp, wplg = w_pl_proj.astype(BF16), w_pl_gate.astype(BF16)

    def layer_weights(i):
        return dict(w_big=w_big, w_small=w_small, par=par, conv_w=gdn_conv_w[i],
                    norm_w=gdn_norm_w[i], wfox=wfox, wgdn=wgdn, wout=wout,
                    wplp=wplp, wplg=wplg, vec=vec)

    lp_s = -(-dseq // GDN_CHUNK) * GDN_CHUNK
    xs = jnp.pad(x_sample, ((0, 0), (0, lp_s - dseq), (0, 0))).reshape(bs * lp_s, d)
    ps_all = jnp.pad(p_sample, ((0, 0), (0, 0), (0, lp_s - dseq), (0, 0))).reshape(depth, bs * lp_s, ple)
    xp = x_prompt.reshape(bp * seq, d)
    pp_all = p_prompt.reshape(depth, bp * seq, ple)

    cp_rows = _cumsum_rows(cache_fox_logf.transpose(0, 1, 3, 2).reshape(depth * bs * HEADS, past))
    cp_rows = cp_rows.reshape(depth, bs, HEADS, past)
    cp_flat = cp_rows.transpose(0, 1, 3, 2).reshape(depth, bs, 1, past * HEADS)
    cp_last = cp_rows[..., past - 1:].reshape(depth, bs * HEADS, 1)

    conv0_p = jnp.zeros((bp, CONV_W - 1, CONV_CH), F32)
    s0_p = jnp.zeros((bp, HEADS, HEAD_DIM, HEAD_DIM), F32)
    kp = lax.empty((depth, bp * seq, WIDTH), F32)
    vp = lax.empty((depth, bp * seq, WIDTH), F32)
    ks = lax.empty((depth, bs * lp_s, WIDTH), F32)
    vs = lax.empty((depth, bs * lp_s, WIDTH), F32)
    outs_p, outs_s = [], []
    for i in range(depth):
        wts = layer_weights(i)
        xp, kp, vp, *rest_p = _layer(xp, bp, seq, pp_all, i, wts, conv0_p, s0_p, alpha, kp, vp)
        xs, ks, vs, *rest_s = _layer(xs, bs, dseq, ps_all, i, wts, state_gdn_conv[i], state_gdn[i], alpha,
                                     ks, vs, fox_cache=(cache_fox_k, cache_fox_v, cp_flat, cp_last))
        outs_p.append(rest_p)
        outs_s.append(rest_s)
    stack = lambda outs, j: jnp.stack([o[j] for o in outs])
    heads_p = lambda a: a.reshape(depth, bp, seq, HEADS, HEAD_DIM)
    heads_s = lambda a: a.reshape(depth, bs, lp_s, HEADS, HEAD_DIM)[:, :, :dseq]
    y_prompt = xp.reshape(bp, seq, d)
    y_sample = xs.reshape(bs, lp_s, d)[:, :dseq]
    return (y_prompt, y_sample,
            heads_p(kp), heads_p(vp), stack(outs_p, 0), stack(outs_p, 1), stack(outs_p, 2),
            heads_s(ks), heads_s(vs), stack(outs_s, 0), stack(outs_s, 1), stack(outs_s, 2))
```

```python
import functools
import math

import jax
import jax.numpy as jnp
from jax import lax
from jax.experimental import pallas as pl
from jax.experimental.pallas import tpu as pltpu

F32 = jnp.float32
BF16 = jnp.bfloat16

HEADS = 8
HEAD_DIM = 128
WIDTH = HEADS * HEAD_DIM
CONV_W = 4
CONV_CH = 3 * WIDTH
LN_EPS = 1e-5
RMS_EPS = 1e-6
LOG2E = math.log2(math.e)
LANES = 128
SUBLANES = 8
GDN_CHUNK = 128
GDN_BATCHES = 2
INV_BASE = 8
FOX_BQ = 2048
FOX_BK = 1024
FOX_ROWS = 256
FOX_HEADS_PER_STEP = 1
FOX_STRIP = 16
VMEM_LIMIT = 56 * 1024 * 1024

Z_Q, Z_GATE_A, Z_QKV_B, Z_GATE_B, Z_M_A, Z_M_B, Z_END = 0, 1, 2, 5, 6, 8, 10


def _params(*sem):
    return pltpu.CompilerParams(dimension_semantics=sem, vmem_limit_bytes=VMEM_LIMIT)


def _mm(a, b):
    return jnp.dot(a.astype(BF16), b.astype(BF16), preferred_element_type=F32)


def _mm_nt(a, b):
    return lax.dot_general(a.astype(BF16), b.astype(BF16), (((1,), (1,)), ((), ())),
                           preferred_element_type=F32)


def _mm_tn(a, b):
    return lax.dot_general(a.astype(BF16), b.astype(BF16), (((0,), (0,)), ((), ())),
                           preferred_element_type=F32)


def _sigmoid(x):
    return 0.5 * jnp.tanh(0.5 * x) + 0.5


def _silu(x):
    return x * _sigmoid(x)


def _proj_kernel(x_ref, w_ref, ws_ref, par_ref, k_in_ref, v_in_ref, z_ref, s_ref, k_ref, v_ref, xb_ref,
                 *, z_tiles, kv_tiles):
    del k_in_ref, v_in_ref
    j = pl.program_id(1)

    @pl.when(j == 0)
    def _():
        x = x_ref[...]
        xb = x.astype(BF16)
        xb_ref[...] = xb
        x_lo = (x - xb.astype(F32)).astype(BF16)
        raw = (jnp.dot(xb, ws_ref[0], preferred_element_type=F32)
               + jnp.dot(x_lo, ws_ref[0], preferred_element_type=F32)
               + jnp.dot(xb, ws_ref[1], preferred_element_type=F32))
        y = raw + par_ref[0:1, :]
        neg_a = -jnp.exp(par_ref[1:2, :])
        t = jnp.log1p(jnp.exp(-jnp.abs(y)))
        logf = jnp.minimum(y, 0.0) - t
        g = neg_a * (jnp.maximum(y, 0.0) + t)
        beta = _sigmoid(raw)
        lane = lax.broadcasted_iota(jnp.int32, raw.shape, 1)
        s_ref[...] = jnp.where(lane < HEADS, logf,
                               jnp.where(lane < 2 * HEADS, g,
                                         jnp.where(lane < 3 * HEADS, beta, 0.0)))

    @pl.when(j < z_tiles)
    def _():
        z_ref[...] = _mm_nt(xb_ref[...], w_ref[...])

    @pl.when((j >= z_tiles) & (j < z_tiles + kv_tiles))
    def _():
        k_ref[...] = _mm_nt(xb_ref[...], w_ref[...])

    @pl.when(j >= z_tiles + kv_tiles)
    def _():
        v_ref[...] = _mm_nt(xb_ref[...], w_ref[...])


def _proj(x2d, w_big, w_small, par, k_all, v_all, layer):
    t_rows, d = x2d.shape
    tm = min(1024, t_rows)
    tn = 512
    kv_tiles = WIDTH // tn
    z_tiles = w_big.shape[1] // tn - 2 * kv_tiles
    clamp = lambda j, lo: jnp.clip(j - lo, 0, kv_tiles - 1)
    return pl.pallas_call(
        functools.partial(_proj_kernel, z_tiles=z_tiles, kv_tiles=kv_tiles),
        grid=(t_rows // tm, z_tiles + 2 * kv_tiles),
        in_specs=[
            pl.BlockSpec((tm, d), lambda i, j: (i, 0)),
            pl.BlockSpec((None, tn, d), lambda i, j: (layer, j, 0)),
            pl.BlockSpec((None, 2, d, LANES), lambda i, j: (layer, 0, 0, 0)),
            pl.BlockSpec((None, SUBLANES, LANES), lambda i, j: (layer, 0, 0)),
            pl.BlockSpec(memory_space=pl.ANY),
            pl.BlockSpec(memory_space=pl.ANY),
        ],
        out_specs=[
            pl.BlockSpec((tm, tn), lambda i, j: (i, jnp.minimum(j, z_tiles - 1))),
            pl.BlockSpec((tm, LANES), lambda i, j: (i, 0)),
            pl.BlockSpec((None, tm, tn), lambda i, j: (layer, i, clamp(j, z_tiles))),
            pl.BlockSpec((None, tm, tn), lambda i, j: (layer, i, clamp(j, z_tiles + kv_tiles))),
        ],
        out_shape=[
            jax.ShapeDtypeStruct((t_rows, z_tiles * tn), F32),
            jax.ShapeDtypeStruct((t_rows, LANES), F32),
            jax.ShapeDtypeStruct(k_all.shape, F32),
            jax.ShapeDtypeStruct(v_all.shape, F32),
        ],
        input_output_aliases={4: 2, 5: 3},
        scratch_shapes=[pltpu.VMEM((tm, d), BF16)],
        compiler_params=_params("arbitrary", "arbitrary"),
        name="proj",
    )(x2d, w_big, w_small, par, k_all, v_all)


def _cumsum_kernel(x_ref, init_ref, o_ref, carry_ref, *, blk, period):
    r = lax.broadcasted_iota(jnp.int32, (blk, blk), 0)
    c = lax.broadcasted_iota(jnp.int32, (blk, blk), 1)
    keep = r <= c
    if period is not None:
        keep = keep & ((r // period) == (c // period))
    y = jnp.dot(x_ref[...], keep.astype(F32), precision=lax.Precision.HIGHEST,
                preferred_element_type=F32)
    if period is None:
        @pl.when(pl.program_id(0) == 0)
        def _():
            carry_ref[...] = init_ref[...]

        y = y + carry_ref[...]
        carry_ref[...] = y[:, blk - 1:blk]
    o_ref[...] = y


def _cumsum_rows(x, period=None, init=None):
    rows, length = x.shape
    blk = min(512, length)
    assert length % blk == 0 and (period is None or (blk % period == 0 and init is None))
    if init is None:
        init = jnp.zeros((rows, 1), F32)
    return pl.pallas_call(
        functools.partial(_cumsum_kernel, blk=blk, period=period),
        grid=(length // blk,),
        in_specs=[pl.BlockSpec((rows, blk), lambda j: (0, j)), pl.BlockSpec((rows, 1), lambda j: (0, 0))],
        out_specs=pl.BlockSpec((rows, blk), lambda j: (0, j)),
        out_shape=jax.ShapeDtypeStruct((rows, length), F32),
        scratch_shapes=[pltpu.VMEM((rows, 1), F32)],
        compiler_params=_params("arbitrary"),
        name="cumsum",
    )(x, init)


def _gates_kernel(s_ref, init_ref, lf_ref, c_ref, gc_ref, cols_ref, carry_ref, *, blk, period, n_tok):
    j = pl.program_id(1)

    @pl.when(j == 0)
    def _():
        carry_ref[...] = init_ref[...]

    x = s_ref[...]
    pos = j * blk + lax.broadcasted_iota(jnp.int32, x.shape, 0)
    lane = lax.broadcasted_iota(jnp.int32, x.shape, 1)
    x = jnp.where((pos >= n_tok) & (lane >= HEADS), 0.0, x)
    xt = x.T
    r = lax.broadcasted_iota(jnp.int32, (blk, blk), 0)
    c = lax.broadcasted_iota(jnp.int32, (blk, blk), 1)
    upper = r <= c
    local = upper & ((r // period) == (c // period))
    rows = xt[0:2 * HEADS]
    run = jnp.dot(rows, upper.astype(F32), precision=lax.Precision.HIGHEST, preferred_element_type=F32)
    loc = jnp.dot(rows, local.astype(F32), precision=lax.Precision.HIGHEST, preferred_element_type=F32)
    csum = run[0:HEADS] + carry_ref[...]
    carry_ref[...] = csum[:, blk - 1:blk]
    gc = loc[HEADS:2 * HEADS]
    lf_ref[...] = xt[0:HEADS]
    c_ref[...] = csum
    gc_ref[...] = gc
    cols_t = jnp.concatenate([gc, xt[2 * HEADS:3 * HEADS], jnp.zeros((LANES - 2 * HEADS, blk), F32)], axis=0)
    cols_ref[...] = cols_t.T


def _gates(small3, n_tok, init):
    bsz, lp, _ = small3.shape
    blk = min(512, lp)
    assert lp % blk == 0 and blk % GDN_CHUNK == 0
    rows_blk = pl.BlockSpec((None, HEADS, blk), lambda b, j: (b, 0, j))
    return pl.pallas_call(
        functools.partial(_gates_kernel, blk=blk, period=GDN_CHUNK, n_tok=n_tok),
        grid=(bsz, lp // blk),
        in_specs=[pl.BlockSpec((None, blk, LANES), lambda b, j: (b, j, 0)),
                  pl.BlockSpec((None, HEADS, 1), lambda b, j: (b, 0, 0))],
        out_specs=[rows_blk, rows_blk, rows_blk, pl.BlockSpec((None, blk, LANES), lambda b, j: (b, j, 0))],
        out_shape=[jax.ShapeDtypeStruct((bsz, HEADS, lp), F32)] * 3
                  + [jax.ShapeDtypeStruct((bsz, lp, LANES), F32)],
        scratch_shapes=[pltpu.VMEM((HEADS, 1), F32)],
        compiler_params=_params("arbitrary", "arbitrary"),
        name="gates",
    )(small3, init)


def _fox_kernel(q_ref, k_ref, v_ref, c_ref, ga_ref, o_ref, kb_ref, vb_ref, *, bq, bk, rows, scale):
    qi = pl.program_id(2)
    length = k_ref.shape[0]
    nh = kb_ref.shape[0]
    head = lambda hh: slice(hh * HEAD_DIM, (hh + 1) * HEAD_DIM)
    units = [(hh, r * rows) for hh in range(nh) for r in range(bq // rows)]

    @pl.when(qi == 0)
    def _():
        for hh in range(nh):
            kb_ref[hh] = k_ref[:, head(hh)].astype(BF16)
            vb_ref[hh, :, :HEAD_DIM] = v_ref[:, head(hh)].astype(BF16)
            vb_ref[hh, :, HEAD_DIM:] = jnp.ones((length, LANES), BF16)

    q = (q_ref[...] * (scale * LOG2E)).astype(BF16)
    qs = [q[r0:r0 + rows, head(hh)] for hh, r0 in units]

    def update(kj, carries, diag_offset=None):
        start = pl.multiple_of(kj * bk, bk)
        widths = {u: bk if diag_offset is None else min(bk, r0 + rows - diag_offset)
                  for u, (_, r0) in enumerate(units)}
        widths = {u: w for u, w in widths.items() if w > 0}
        keyed = sorted({(units[u][0], w) for u, w in widths.items()})
        kblks = {(hh, w): kb_ref[hh, pl.ds(start, w), :] for hh, w in keyed}
        cks = {(hh, w): c_ref[hh, kj, :, 0:w] * LOG2E for hh, w in keyed}
        ss = {u: lax.dot_general(qs[u], kblks[units[u][0], w], (((1,), (1,)), ((), ())),
                                 preferred_element_type=F32)
              for u, w in widths.items()}
        out = list(carries)
        for u, s in ss.items():
            m, acc, l = carries[u]
            hh, r0 = units[u]
            width = widths[u]
            lane_blocks = [slice(c * LANES, (c + 1) * LANES) for c in range(width // LANES)]
            masked = diag_offset is not None and diag_offset + width - 1 > r0
            m_parts, a_parts, p_parts = [], [], []
            for i in range(rows // FOX_STRIP):
                sl = slice(i * FOX_STRIP, (i + 1) * FOX_STRIP)
                t = s[sl] - cks[hh, width]
                if masked:
                    row = lax.broadcasted_iota(jnp.int32, (FOX_STRIP, width), 0)
                    col = lax.broadcasted_iota(jnp.int32, (FOX_STRIP, width), 1)
                    t = jnp.where(col + (diag_offset - r0 - i * FOX_STRIP) <= row, t, -jnp.inf)
                mx = t[:, lane_blocks[0]]
                for c in lane_blocks[1:]:
                    mx = jnp.maximum(mx, t[:, c])
                m_new = jnp.maximum(m[sl], jnp.broadcast_to(jnp.max(mx, axis=1, keepdims=True), mx.shape))
                m_parts.append(m_new)
                a_parts.append(jnp.exp2(m[sl] - m_new))
                p_parts.append(jnp.concatenate([jnp.exp2(t[:, c] - m_new).astype(BF16) for c in lane_blocks],
                                               axis=1))
            alpha = jnp.concatenate(a_parts, axis=0)
            pv = jnp.dot(jnp.concatenate(p_parts, axis=0), vb_ref[hh, pl.ds(start, width), :],
                         preferred_element_type=F32)
            out[u] = (jnp.concatenate(m_parts, axis=0), alpha * acc + pv[:, :HEAD_DIM],
                      alpha * l + pv[:, HEAD_DIM:])
        return tuple(out)

    init = (jnp.full((rows, LANES), -jnp.inf, F32), jnp.zeros((rows, HEAD_DIM), F32),
            jnp.zeros((rows, LANES), F32))
    per_q = bq // bk
    def past_q_block(j, cr):
        for d in range(per_q):
            cr = update(j * per_q + d, cr)
        return cr

    carries = lax.fori_loop(0, qi, past_q_block, (init,) * len(units))
    for d in range(per_q):
        carries = update(qi * per_q + d, carries, diag_offset=d * bk)
    for (hh, r0), (_, acc, l) in zip(units, carries):
        rs = slice(r0, r0 + rows)
        o_ref[rs, head(hh)] = (acc / l * _silu(ga_ref[rs, head(hh)])).astype(BF16)


def _fox_prompt(z3, k_all, v_all, layer, c_rows):
    bsz, length, _ = z3.shape
    bq = min(FOX_BQ, length)
    bk = min(FOX_BK, bq)
    rows = min(FOX_ROWS, bq)
    assert length % bq == 0 and bq % bk == 0 and bq % rows == 0
    nk = length // bk
    c5 = c_rows.reshape(bsz, HEADS, nk, 1, bk)
    nh = FOX_HEADS_PER_STEP
    hw = nh * HEAD_DIM
    assert HEADS % nh == 0 and WIDTH % hw == 0
    seg = lambda s: s * (WIDTH // hw)
    return pl.pallas_call(
        functools.partial(_fox_kernel, bq=bq, bk=bk, rows=rows, scale=HEAD_DIM ** -0.5),
        grid=(bsz, HEADS // nh, length // bq),
        in_specs=[
            pl.BlockSpec((None, bq, hw), lambda b, h, i: (b, i, seg(Z_Q) + h)),
            pl.BlockSpec((None, None, length, hw), lambda b, h, i: (layer, b, 0, h)),
            pl.BlockSpec((None, None, length, hw), lambda b, h, i: (layer, b, 0, h)),
            pl.BlockSpec((None, nh, nk, 1, bk), lambda b, h, i: (b, h, 0, 0, 0)),
            pl.BlockSpec((None, bq, hw), lambda b, h, i: (b, i, seg(Z_GATE_A) + h)),
        ],
        out_specs=pl.BlockSpec((None, bq, hw), lambda b, h, i: (b, i, h)),
        out_shape=jax.ShapeDtypeStruct((bsz, length, WIDTH), BF16),
        scratch_shapes=[pltpu.VMEM((nh, length, HEAD_DIM), BF16),
                        pltpu.VMEM((nh, length, 2 * HEAD_DIM), BF16)],
        compiler_params=_params("arbitrary", "arbitrary", "arbitrary"),
        name="fox_prompt",
    )(z3, k_all, v_all, c5, z3)


def _fox_decode_kernel(q_ref, kn_ref, vn_ref, cn_ref, ga_ref, kp_ref, vp_ref, cp_ref, o_ref,
                       m_ref, acc_ref, l_ref, *, n, scale):
    j = pl.program_id(1)
    nrow = q_ref.shape[0]
    q = (q_ref[...] * (scale * LOG2E)).astype(BF16)

    @pl.when(j == 0)
    def _():
        m_ref[...] = jnp.full(m_ref.shape, -jnp.inf, F32)
        acc_ref[...] = jnp.zeros(acc_ref.shape, F32)
        l_ref[...] = jnp.zeros(l_ref.shape, F32)

    def update(kblk, vblk, c_row, visible):
        t = jnp.where(visible, _mm_nt(q, kblk) - c_row * LOG2E, -jnp.inf)
        m = m_ref[...]
        m_new = jnp.maximum(m, jnp.broadcast_to(jnp.max(t, axis=1, keepdims=True), m.shape))
        alpha = jnp.exp2(m - m_new)
        p = jnp.concatenate([jnp.exp2(t[:, c * LANES:(c + 1) * LANES] - m_new)
                             for c in range(t.shape[1] // LANES)], axis=1)
        m_ref[...] = m_new
        l_ref[...] = alpha * l_ref[...] + jnp.broadcast_to(jnp.sum(p, axis=1, keepdims=True), m.shape)
        acc_ref[...] = alpha * acc_ref[...] + _mm(p, vblk)

    cols = kp_ref.shape[0]
    row_head = lax.broadcasted_iota(jnp.int32, (nrow, cols), 0) // n
    col_head = lax.broadcasted_iota(jnp.int32, (nrow, cols), 1) % HEADS
    update(kp_ref[...], vp_ref[...], cp_ref[...], row_head == col_head)

    @pl.when(j == pl.num_programs(1) - 1)
    def _():
        r = lax.broadcasted_iota(jnp.int32, (nrow, nrow), 0)
        c = lax.broadcasted_iota(jnp.int32, (nrow, nrow), 1)
        update(kn_ref[...], vn_ref[...], cn_ref[...], (r // n == c // n) & (c % n <= r % n))
        o_ref[...] = (acc_ref[...] / l_ref[...] * _silu(ga_ref[...])).astype(BF16)


def _fox_decode(z3, k4, v4, n_tok, cache_k, cache_v, layer, cp_flat, cn_rows):
    bsz = z3.shape[0]
    past = cache_k.shape[2]
    nrow = HEADS * n_tok
    pc = min(1024, past)
    assert past % pc == 0 and nrow % LANES == 0

    def head_rows(a):
        return a.reshape(bsz, n_tok, HEADS, HEAD_DIM).transpose(0, 2, 1, 3).reshape(bsz, nrow, HEAD_DIM)

    seg = lambda s: z3[:, :n_tok, s * WIDTH:(s + 1) * WIDTH]
    k2 = cache_k.reshape(cache_k.shape[:2] + (past * HEADS, HEAD_DIM))
    v2 = cache_v.reshape(cache_v.shape[:2] + (past * HEADS, HEAD_DIM))
    rows_blk = pl.BlockSpec((None, nrow, HEAD_DIM), lambda b, j: (b, 0, 0))
    cache_blk = pl.BlockSpec((None, None, pc * HEADS, HEAD_DIM), lambda b, j: (layer, b, j, 0))
    o_rows = pl.pallas_call(
        functools.partial(_fox_decode_kernel, n=n_tok, scale=HEAD_DIM ** -0.5),
        grid=(bsz, past // pc),
        in_specs=[
            rows_blk, rows_blk, rows_blk,
            pl.BlockSpec((None, 1, nrow), lambda b, j: (b, 0, 0)),
            rows_blk, cache_blk, cache_blk,
            pl.BlockSpec((None, None, 1, pc * HEADS), lambda b, j: (layer, b, 0, j)),
        ],
        out_specs=rows_blk,
        out_shape=jax.ShapeDtypeStruct((bsz, nrow, HEAD_DIM), BF16),
        scratch_shapes=[pltpu.VMEM((nrow, LANES), F32), pltpu.VMEM((nrow, HEAD_DIM), F32),
                        pltpu.VMEM((nrow, LANES), F32)],
        compiler_params=_params("arbitrary", "arbitrary"),
        name="fox_decode",
    )(head_rows(seg(Z_Q)), head_rows(k4[layer, :, :n_tok]), head_rows(v4[layer, :, :n_tok]),
      cn_rows.reshape(bsz, 1, nrow), head_rows(seg(Z_GATE_A)), k2, v2, cp_flat)
    return o_rows.reshape(bsz, HEADS, n_tok, HEAD_DIM).transpose(0, 2, 1, 3).reshape(bsz, n_tok, WIDTH)


def _gdn_kernel(qp_ref, kp_ref, vp_ref, cw_ref, c0_ref, cols_ref, gr_ref, gb_ref, nw_ref, s0_ref,
                og_ref, s_ref, xp_ref, *, chunk, scale):
    pad = SUBLANES
    nb = qp_ref.shape[0]

    @pl.when(pl.program_id(1) == 0)
    def _():
        s_ref[...] = s0_ref[...]
        xp_ref[:, 0:pad, :] = c0_ref[...]

    convs = []
    for bb in range(nb):
        xp_ref[bb, pad:pad + chunk, 0:WIDTH] = qp_ref[bb]
        xp_ref[bb, pad:pad + chunk, WIDTH:2 * WIDTH] = kp_ref[bb]
        xp_ref[bb, pad:pad + chunk, 2 * WIDTH:3 * WIDTH] = vp_ref[bb]
        xp = xp_ref[bb]
        acc = xp * cw_ref[0:1, :]
        for i in range(1, CONV_W):
            acc = pltpu.roll(acc, 1, axis=0) + xp * cw_ref[i:i + 1, :]
        convs.append(_silu(acc[pad:pad + chunk]))
        xp_ref[bb, 0:pad, :] = xp_ref[bb, chunk:chunk + pad, :]

    row = lax.broadcasted_iota(jnp.int32, (chunk, chunk), 0)
    col = lax.broadcasted_iota(jnp.int32, (chunk, chunk), 1)
    incl = row >= col
    strict = row > col
    sizes = [INV_BASE << i for i in range(int(math.log2(chunk // INV_BASE)) + 1)]
    same = [(row >> int(math.log2(b))) == (col >> int(math.log2(b))) for b in sizes[:-1]]
    diag_mask = strict & same[0]
    off_masks = [strict & ~same[i] & (same[i + 1] if i + 1 < len(same) else True)
                 for i in range(len(same))]

    units = [(bb, hh) for bb in range(nb) for hh in range(HEADS)]
    heads = range(len(units))
    sl = lambda seg, hh: slice(seg * WIDTH + hh * HEAD_DIM, seg * WIDTH + (hh + 1) * HEAD_DIM)
    q, k, v, gc, beta, e, lmat, qk = ([None] * len(units) for _ in range(8))
    for h, (bb, hh) in enumerate(units):
        qh, kh = convs[bb][:, sl(0, hh)], convs[bb][:, sl(1, hh)]
        q[h] = qh * (lax.rsqrt(jnp.sum(qh * qh, axis=1, keepdims=True) + RMS_EPS) * scale)
        k[h] = kh * lax.rsqrt(jnp.sum(kh * kh, axis=1, keepdims=True) + RMS_EPS)
        v[h] = convs[bb][:, sl(2, hh)]
        gc[h] = cols_ref[bb, :, hh:hh + 1]
        beta[h] = cols_ref[bb, :, HEADS + hh:HEADS + hh + 1]
        e[h] = jnp.exp(gc[h])
    for h, (bb, hh) in enumerate(units):
        decay = jnp.exp(jnp.where(incl, gc[h] - gr_ref[bb, hh:hh + 1, :], -jnp.inf))
        kq = _mm_nt(jnp.concatenate([k[h] * beta[h], q[h]], axis=0), k[h])
        lmat[h] = jnp.where(strict, kq[:chunk] * decay, 0.0)
        qk[h] = jnp.where(incl, kq[chunk:] * decay, 0.0)
    xpow = [jnp.where(diag_mask, lmat[h], 0.0) for h in heads]
    nm = [-xpow[h] for h in heads]
    for _ in range(int(math.log2(INV_BASE)) - 1):
        xpow = [_mm(xpow[h], xpow[h]) for h in heads]
        nm = [nm[h] + xpow[h] + _mm(nm[h], xpow[h]) for h in heads]
    for off in off_masks:
        loff = [jnp.where(off, lmat[h], 0.0) for h in heads]
        t_loff = [loff[h] + _mm(nm[h], loff[h]) for h in heads]
        nm = [nm[h] - (t_loff[h] + _mm(t_loff[h], nm[h])) for h in heads]
    uw = []
    for h in heads:
        rhs = jnp.concatenate([v[h] * beta[h], k[h] * (beta[h] * e[h])], axis=1)
        uw.append(rhs + _mm(nm[h], rhs))
    s_old = [s_ref[bb, hh] for bb, hh in units]
    ws_qs = [_mm(jnp.concatenate([uw[h][:, HEAD_DIM:], q[h] * e[h]], axis=0), s_old[h]) for h in heads]
    v_new = [uw[h][:, :HEAD_DIM] - ws_qs[h][:chunk] for h in heads]
    for h, (bb, hh) in enumerate(units):
        g_last = gc[h][chunk - 1:chunk, :]
        k_tail = k[h] * jnp.exp(g_last - gc[h])
        s_ref[bb, hh] = s_old[h] * jnp.exp(g_last) + _mm_tn(k_tail, v_new[h])
    for h, (bb, hh) in enumerate(units):
        o = ws_qs[h][chunk:] + _mm(qk[h], v_new[h])
        o = o * lax.rsqrt(jnp.mean(o * o, axis=1, keepdims=True) + RMS_EPS) * nw_ref[...]
        og_ref[bb, :, sl(0, hh)] = (o * _silu(gb_ref[bb, :, sl(0, hh)])).astype(BF16)


def _gdn(z3, conv_w, conv0, cols, gc_rows, norm_w, s0):
    bsz, length, _ = z3.shape
    chunk = GDN_CHUNK
    assert length % chunk == 0
    nc = length // chunk
    nb = GDN_BATCHES if bsz % GDN_BATCHES == 0 else 1
    c0 = jnp.pad(conv0, ((0, 0), (SUBLANES - (CONV_W - 1), 0), (0, 0)))
    zblk = lambda seg: pl.BlockSpec((nb, chunk, WIDTH), lambda b, l: (b, l, seg))
    return pl.pallas_call(
        functools.partial(_gdn_kernel, chunk=chunk, scale=HEAD_DIM ** -0.5),
        grid=(bsz // nb, nc),
        in_specs=[
            zblk(Z_QKV_B), zblk(Z_QKV_B + 1), zblk(Z_QKV_B + 2),
            pl.BlockSpec((CONV_W, CONV_CH), lambda b, l: (0, 0)),
            pl.BlockSpec((nb, SUBLANES, CONV_CH), lambda b, l: (b, 0, 0)),
            pl.BlockSpec((nb, chunk, LANES), lambda b, l: (b, l, 0)),
            pl.BlockSpec((nb, HEADS, chunk), lambda b, l: (b, 0, l)),
            zblk(Z_GATE_B),
            pl.BlockSpec((1, HEAD_DIM), lambda b, l: (0, 0)),
            pl.BlockSpec((nb, HEADS, HEAD_DIM, HEAD_DIM), lambda b, l: (b, 0, 0, 0)),
        ],
        out_specs=[
            pl.BlockSpec((nb, chunk, WIDTH), lambda b, l: (b, l, 0)),
            pl.BlockSpec((nb, HEADS, HEAD_DIM, HEAD_DIM), lambda b, l: (b, 0, 0, 0)),
        ],
        out_shape=[
            jax.ShapeDtypeStruct((bsz, length, WIDTH), BF16),
            jax.ShapeDtypeStruct((bsz, HEADS, HEAD_DIM, HEAD_DIM), F32),
        ],
        scratch_shapes=[pltpu.VMEM((nb, chunk + SUBLANES, CONV_CH), F32)],
        compiler_params=_params("arbitrary", "arbitrary"),
        name="gdn",
    )(z3, z3, z3, conv_w, c0, cols, gc_rows, z3, norm_w.reshape(1, HEAD_DIM), s0)


def _out_kernel(oa_ref, ob_ref, ma_ref, mb_ref, x_ref, p_ref, wfox_ref, wgdn_ref, wout_ref,
                wplp_ref, wplg_ref, vec_ref, o_ref, *, alpha):
    ya = jnp.dot(oa_ref[...], wfox_ref[...], preferred_element_type=F32)
    yb = jnp.dot(ob_ref[...], wgdn_ref[...], preferred_element_type=F32)
    merged = _sigmoid(ma_ref[...]) * ya + _sigmoid(mb_ref[...]) * yb
    y = alpha * x_ref[...] + _mm(merged, wout_ref[...])
    yc = y - jnp.mean(y, axis=1, keepdims=True)
    var = jnp.mean(yc * yc, axis=1, keepdims=True)
    x1 = yc * lax.rsqrt(var + LN_EPS) * vec_ref[0:1, :] + vec_ref[1:2, :]
    e = _mm(p_ref[...], wplp_ref[...])
    e = e * lax.rsqrt(jnp.mean(e * e, axis=1, keepdims=True) + RMS_EPS) * vec_ref[2:3, :]
    o_ref[...] = x1 + _sigmoid(_mm(x1, wplg_ref[...])) * e


def _out_block(oa, ob, z2d, x2d, p_all, layer, wfox, wgdn, wout, wplp, wplg, vec, alpha):
    t_rows, d = x2d.shape
    ple = p_all.shape[-1]
    tm = min(256, t_rows)
    dpw = d // WIDTH
    assert d % WIDTH == 0 and Z_M_A % dpw == 0 and Z_M_B % dpw == 0
    const = lambda shape: pl.BlockSpec((None,) + shape, lambda r: (layer,) + (0,) * len(shape),
                                       pipeline_mode=pl.Buffered(1))
    return pl.pallas_call(
        functools.partial(_out_kernel, alpha=alpha),
        grid=(t_rows // tm,),
        in_specs=[
            pl.BlockSpec((tm, WIDTH), lambda r: (r, 0)),
            pl.BlockSpec((tm, WIDTH), lambda r: (r, 0)),
            pl.BlockSpec((tm, d), lambda r: (r, Z_M_A // dpw)),
            pl.BlockSpec((tm, d), lambda r: (r, Z_M_B // dpw)),
            pl.BlockSpec((tm, d), lambda r: (r, 0)),
            pl.BlockSpec((None, tm, ple), lambda r: (layer, r, 0)),
            const((WIDTH, d)), const((WIDTH, d)), const((d, d)), const((ple, d)), const((d, d)),
            const((SUBLANES, d)),
        ],
        out_specs=pl.BlockSpec((tm, d), lambda r: (r, 0)),
        out_shape=jax.ShapeDtypeStruct((t_rows, d), F32),
        compiler_params=_params("arbitrary"),
        name="out_block",
    )(oa, ob, z2d, z2d, x2d, p_all, wfox, wgdn, wout, wplp, wplg, vec)


def _layer(x2d, bsz, n_tok, p_all, layer, wts, conv0, s0, alpha, k_all, v_all, fox_cache=None):
    lp = x2d.shape[0] // bsz
    z2d, small, k_all, v_all = _proj(x2d, wts["w_big"], wts["w_small"], wts["par"], k_all, v_all, layer)
    z3 = z2d.reshape(bsz, lp, -1)
    k4 = k_all.reshape(-1, bsz, lp, WIDTH)
    v4 = v_all.reshape(-1, bsz, lp, WIDTH)
    small3 = small.reshape(bsz, lp, LANES)
    c_init = (jnp.zeros((bsz, HEADS, 1), F32) if fox_cache is None
              else fox_cache[3][layer].reshape(bsz, HEADS, 1))
    logf_rows, c_rows, gc_rows, cols = _gates(small3, n_tok, c_init)
    logf = logf_rows[:, :, :n_tok].transpose(0, 2, 1)

    if fox_cache is None:
        oa = _fox_prompt(z3, k4, v4, layer, c_rows)
    else:
        cache_k, cache_v, cp_flat, _ = fox_cache
        oa = _fox_decode(z3, k4, v4, n_tok, cache_k, cache_v, layer, cp_flat, c_rows[:, :, :n_tok])
        oa = jnp.pad(oa, ((0, 0), (0, lp - n_tok), (0, 0)))
    ob, s_new = _gdn(z3, wts["conv_w"], conv0, cols, gc_rows, wts["norm_w"], s0)

    x_out = _out_block(oa.reshape(bsz * lp, WIDTH), ob.reshape(bsz * lp, WIDTH), z2d, x2d, p_all, layer,
                       wts["wfox"], wts["wgdn"], wts["wout"], wts["wplp"], wts["wplg"], wts["vec"], alpha)

    new_conv = z3[:, n_tok - (CONV_W - 1):n_tok, Z_QKV_B * WIDTH:Z_QKV_B * WIDTH + CONV_CH]
    return x_out, k_all, v_all, logf, new_conv, s_new


def kernel(x_prompt, x_sample, cache_fox_k, cache_fox_v, cache_fox_logf, state_gdn_conv, state_gdn, p_prompt, p_sample, w_in, fox_f_bias, gdn_conv_w, gdn_a_log, gdn_dt_bias, gdn_norm_w, w_out_fox, w_out_gdn, w_out, ln_g, ln_b, w_pl_proj, w_pl_gate, pl_norm_w):
    depth = w_in.shape[0]
    bp, seq, d = x_prompt.shape
    bs, dseq, _ = x_sample.shape
    past = cache_fox_k.shape[2]
    ple = p_prompt.shape[-1]
    assert d % WIDTH == 0 and dseq >= CONV_W - 1 and seq >= CONV_W - 1
    alpha = (2 * depth) ** 0.25

    o_f = 3 * WIDTH
    o_ga = o_f + HEADS
    o_qkv = o_ga + WIDTH
    o_a = o_qkv + CONV_CH
    o_gb = o_a + 2 * HEADS
    w_in_t = jnp.swapaxes(w_in, 1, 2).astype(BF16)
    w_big = jnp.concatenate([w_in_t[:, :WIDTH], w_in_t[:, o_ga:o_a], w_in_t[:, o_gb:],
                             w_in_t[:, WIDTH:o_f]], axis=1)
    w_small = jnp.concatenate([w_in[..., o_f:o_ga], w_in[..., o_a:o_gb],
                               jnp.zeros((depth, d, LANES - 3 * HEADS), F32)], axis=-1)
    w_small_hi = w_small.astype(BF16)
    w_small = jnp.stack([w_small_hi, (w_small - w_small_hi.astype(F32)).astype(BF16)], axis=1)
    zpad = jnp.zeros((depth, LANES - 2 * HEADS), F32)
    par = jnp.stack([jnp.concatenate([fox_f_bias, gdn_dt_bias, zpad], axis=-1),
                     jnp.concatenate([jnp.zeros_like(fox_f_bias), gdn_a_log, zpad], axis=-1)], axis=1)
    par = jnp.pad(par, ((0, 0), (0, SUBLANES - 2), (0, 0)))
    vec = jnp.pad(jnp.stack([ln_g, ln_b, pl_norm_w], axis=1), ((0, 0), (0, SUBLANES - 3), (0, 0)))
    wfox, wgdn, wout = w_out_fox.astype(BF16), w_out_gdn.astype(BF16), w_out.astype(BF16)
    wplp, wplg = w_pl_proj.astype(BF16), w_pl_gate.astype(BF16)

    def layer_weights(i):
        return dict(w_big=w_big, w_small=w_small, par=par, conv_w=gdn_conv_w[i],
                    norm_w=gdn_norm_w[i], wfox=wfox, wgdn=wgdn, wout=wout,
                    wplp=wplp, wplg=wplg, vec=vec)

    lp_s = -(-dseq // GDN_CHUNK) * GDN_CHUNK
    xs = jnp.pad(x_sample, ((0, 0), (0, lp_s - dseq), (0, 0))).reshape(bs * lp_s, d)
    ps_all = jnp.pad(p_sample, ((0, 0), (0, 0), (0, lp_s - dseq), (0, 0))).reshape(depth, bs * lp_s, ple)
    xp = x_prompt.reshape(bp * seq, d)
    pp_all = p_prompt.reshape(depth, bp * seq, ple)

    cp_rows = _cumsum_rows(cache_fox_logf.transpose(0, 1, 3, 2).reshape(depth * bs * HEADS, past))
    cp_rows = cp_rows.reshape(depth, bs, HEADS, past)
    cp_flat = cp_rows.transpose(0, 1, 3, 2).reshape(depth, bs, 1, past * HEADS)
    cp_last = cp_rows[..., past - 1:].reshape(depth, bs * HEADS, 1)

    conv0_p = jnp.zeros((bp, CONV_W - 1, CONV_CH), F32)
    s0_p = jnp.zeros((bp, HEADS, HEAD_DIM, HEAD_DIM), F32)
    kp = lax.empty((depth, bp * seq, WIDTH), F32)
    vp = lax.empty((depth, bp * seq, WIDTH), F32)
    ks = lax.empty((depth, bs * lp_s, WIDTH), F32)
    vs = lax.empty((depth, bs * lp_s, WIDTH), F32)
    outs_p, outs_s = [], []
    for i in range(depth):
        wts = layer_weights(i)
        xp, kp, vp, *rest_p = _layer(xp, bp, seq, pp_all, i, wts, conv0_p, s0_p, alpha, kp, vp)
        xs, ks, vs, *rest_s = _layer(xs, bs, dseq, ps_all, i, wts, state_gdn_conv[i], state_gdn[i], alpha,
                                     ks, vs, fox_cache=(cache_fox_k, cache_fox_v, cp_flat, cp_last))
        outs_p.append(rest_p)
        outs_s.append(rest_s)
    stack = lambda outs, j: jnp.stack([o[j] for o in outs])
    heads_p = lambda a: a.reshape(depth, bp, seq, HEADS, HEAD_DIM)
    heads_s = lambda a: a.reshape(depth, bs, lp_s, HEADS, HEAD_DIM)[:, :, :dseq]
    y_prompt = xp.reshape(bp, seq, d)
    y_sample = xs.reshape(bs, lp_s, d)[:, :dseq]
    return (y_prompt, y_sample,
            heads_p(kp), heads_p(vp), stack(outs_p, 0), stack(outs_p, 1), stack(outs_p, 2),
            heads_s(ks), heads_s(vs), stack(outs_s, 0), stack(outs_s, 1), stack(outs_s, 2))
```

```python
import functools
import math

import jax
import jax.numpy as jnp
from jax import lax
from jax.experimental import pallas as pl
from jax.experimental.pallas import tpu as pltpu

F32 = jnp.float32
BF16 = jnp.bfloat16

HEADS = 8
HEAD_DIM = 128
WIDTH = HEADS * HEAD_DIM
CONV_W = 4
CONV_CH = 3 * WIDTH
LN_EPS = 1e-5
RMS_EPS = 1e-6
LOG2E = math.log2(math.e)
LANES = 128
SUBLANES = 8
GDN_CHUNK = 128
GDN_BATCHES = 2
INV_BASE = 8
FOX_BQ = 2048
FOX_BK = 1024
FOX_ROWS = 256
FOX_HEADS_PER_STEP = 1
FOX_STRIP = 16
VMEM_LIMIT = 56 * 1024 * 1024

Z_Q, Z_GATE_A, Z_QKV_B, Z_GATE_B, Z_M_A, Z_M_B, Z_END = 0, 1, 2, 5, 6, 8, 10


def _params(*sem):
    return pltpu.CompilerParams(dimension_semantics=sem, vmem_limit_bytes=VMEM_LIMIT)


def _mm(a, b):
    return jnp.dot(a.astype(BF16), b.astype(BF16), preferred_element_type=F32)


def _mm_nt(a, b):
    return lax.dot_general(a.astype(BF16), b.astype(BF16), (((1,), (1,)), ((), ())),
                           preferred_element_type=F32)


def _mm_tn(a, b):
    return lax.dot_general(a.astype(BF16), b.astype(BF16), (((0,), (0,)), ((), ())),
                           preferred_element_type=F32)


def _sigmoid(x):
    return 0.5 * jnp.tanh(0.5 * x) + 0.5


def _silu(x):
    return x * _sigmoid(x)


def _proj_kernel(x_ref, w_ref, ws_ref, par_ref, k_in_ref, v_in_ref, z_ref, s_ref, k_ref, v_ref, xb_ref,
                 *, z_tiles, kv_tiles):
    del k_in_ref, v_in_ref
    j = pl.program_id(1)

    @pl.when(j == 0)
    def _():
        x = x_ref[...]
        xb = x.astype(BF16)
        xb_ref[...] = xb
        x_lo = (x - xb.astype(F32)).astype(BF16)
        raw = (jnp.dot(xb, ws_ref[0], preferred_element_type=F32)
               + jnp.dot(x_lo, ws_ref[0], preferred_element_type=F32)
               + jnp.dot(xb, ws_ref[1], preferred_element_type=F32))
        y = raw + par_ref[0:1, :]
        neg_a = -jnp.exp(par_ref[1:2, :])
        t = jnp.log1p(jnp.exp(-jnp.abs(y)))
        logf = jnp.minimum(y, 0.0) - t
        g = neg_a * (jnp.maximum(y, 0.0) + t)
        beta = _sigmoid(raw)
        lane = lax.broadcasted_iota(jnp.int32, raw.shape, 1)
        s_ref[...] = jnp.where(lane < HEADS, logf,
                               jnp.where(lane < 2 * HEADS, g,
                                         jnp.where(lane < 3 * HEADS, beta, 0.0)))

    @pl.when(j < z_tiles)
    def _():
        z_ref[...] = _mm_nt(xb_ref[...], w_ref[...])

    @pl.when((j >= z_tiles) & (j < z_tiles + kv_tiles))
    def _():
        k_ref[...] = _mm_nt(xb_ref[...], w_ref[...])

    @pl.when(j >= z_tiles + kv_tiles)
    def _():
        v_ref[...] = _mm_nt(xb_ref[...], w_ref[...])


def _proj(x2d, w_big, w_small, par, k_all, v_all, layer):
    t_rows, d = x2d.shape
    tm = min(1024, t_rows)
    tn = 512
    kv_tiles = WIDTH // tn
    z_tiles = w_big.shape[1] // tn - 2 * kv_tiles
    clamp = lambda j, lo: jnp.clip(j - lo, 0, kv_tiles - 1)
    return pl.pallas_call(
        functools.partial(_proj_kernel, z_tiles=z_tiles, kv_tiles=kv_tiles),
        grid=(t_rows // tm, z_tiles + 2 * kv_tiles),
        in_specs=[
            pl.BlockSpec((tm, d), lambda i, j: (i, 0)),
            pl.BlockSpec((None, tn, d), lambda i, j: (layer, j, 0)),
            pl.BlockSpec((None, 2, d, LANES), lambda i, j: (layer, 0, 0, 0)),
            pl.BlockSpec((None, SUBLANES, LANES), lambda i, j: (layer, 0, 0)),
            pl.BlockSpec(memory_space=pl.ANY),
            pl.BlockSpec(memory_space=pl.ANY),
        ],
        out_specs=[
            pl.BlockSpec((tm, tn), lambda i, j: (i, jnp.minimum(j, z_tiles - 1))),
            pl.BlockSpec((tm, LANES), lambda i, j: (i, 0)),
            pl.BlockSpec((None, tm, tn), lambda i, j: (layer, i, clamp(j, z_tiles))),
            pl.BlockSpec((None, tm, tn), lambda i, j: (layer, i, clamp(j, z_tiles + kv_tiles))),
        ],
        out_shape=[
            jax.ShapeDtypeStruct((t_rows, z_tiles * tn), F32),
            jax.ShapeDtypeStruct((t_rows, LANES), F32),
            jax.ShapeDtypeStruct(k_all.shape, F32),
            jax.ShapeDtypeStruct(v_all.shape, F32),
        ],
        input_output_aliases={4: 2, 5: 3},
        scratch_shapes=[pltpu.VMEM((tm, d), BF16)],
        compiler_params=_params("arbitrary", "arbitrary"),
        name="proj",
    )(x2d, w_big, w_small, par, k_all, v_all)


def _cumsum_kernel(x_ref, init_ref, o_ref, carry_ref, *, blk, period):
    r = lax.broadcasted_iota(jnp.int32, (blk, blk), 0)
    c = lax.broadcasted_iota(jnp.int32, (blk, blk), 1)
    keep = r <= c
    if period is not None:
        keep = keep & ((r // period) == (c // period))
    y = jnp.dot(x_ref[...], keep.astype(F32), precision=lax.Precision.HIGHEST,
                preferred_element_type=F32)
    if period is None:
        @pl.when(pl.program_id(0) == 0)
        def _():
            carry_ref[...] = init_ref[...]

        y = y + carry_ref[...]
        carry_ref[...] = y[:, blk - 1:blk]
    o_ref[...] = y


def _cumsum_rows(x, period=None, init=None):
    rows, length = x.shape
    blk = min(512, length)
    assert length % blk == 0 and (period is None or (blk % period == 0 and init is None))
    if init is None:
        init = jnp.zeros((rows, 1), F32)
    return pl.pallas_call(
        functools.partial(_cumsum_kernel, blk=blk, period=period),
        grid=(length // blk,),
        in_specs=[pl.BlockSpec((rows, blk), lambda j: (0, j)), pl.BlockSpec((rows, 1), lambda j: (0, 0))],
        out_specs=pl.BlockSpec((rows, blk), lambda j: (0, j)),
        out_shape=jax.ShapeDtypeStruct((rows, length), F32),
        scratch_shapes=[pltpu.VMEM((rows, 1), F32)],
        compiler_params=_params("arbitrary"),
        name="cumsum",
    )(x, init)


def _gates_kernel(s_ref, init_ref, lf_ref, c_ref, gc_ref, cols_ref, carry_ref, *, blk, period, n_tok):
    j = pl.program_id(1)

    @pl.when(j == 0)
    def _():
        carry_ref[...] = init_ref[...]

    x = s_ref[...]
    pos = j * blk + lax.broadcasted_iota(jnp.int32, x.shape, 0)
    lane = lax.broadcasted_iota(jnp.int32, x.shape, 1)
    x = jnp.where((pos >= n_tok) & (lane >= HEADS), 0.0, x)
    xt = x.T
    r = lax.broadcasted_iota(jnp.int32, (blk, blk), 0)
    c = lax.broadcasted_iota(jnp.int32, (blk, blk), 1)
    upper = r <= c
    local = upper & ((r // period) == (c // period))
    rows = xt[0:2 * HEADS]
    run = jnp.dot(rows, upper.astype(F32), precision=lax.Precision.HIGHEST, preferred_element_type=F32)
    loc = jnp.dot(rows, local.astype(F32), precision=lax.Precision.HIGHEST, preferred_element_type=F32)
    csum = run[0:HEADS] + carry_ref[...]
    carry_ref[...] = csum[:, blk - 1:blk]
    gc = loc[HEADS:2 * HEADS]
    lf_ref[...] = xt[0:HEADS]
    c_ref[...] = csum
    gc_ref[...] = gc
    cols_t = jnp.concatenate([gc, xt[2 * HEADS:3 * HEADS], jnp.zeros((LANES - 2 * HEADS, blk), F32)], axis=0)
    cols_ref[...] = cols_t.T


def _gates(small3, n_tok, init):
    bsz, lp, _ = small3.shape
    blk = min(512, lp)
    assert lp % blk == 0 and blk % GDN_CHUNK == 0
    rows_blk = pl.BlockSpec((None, HEADS, blk), lambda b, j: (b, 0, j))
    return pl.pallas_call(
        functools.partial(_gates_kernel, blk=blk, period=GDN_CHUNK, n_tok=n_tok),
        grid=(bsz, lp // blk),
        in_specs=[pl.BlockSpec((None, blk, LANES), lambda b, j: (b, j, 0)),
                  pl.BlockSpec((None, HEADS, 1), lambda b, j: (b, 0, 0))],
        out_specs=[rows_blk, rows_blk, rows_blk, pl.BlockSpec((None, blk, LANES), lambda b, j: (b, j, 0))],
        out_shape=[jax.ShapeDtypeStruct((bsz, HEADS, lp), F32)] * 3
                  + [jax.ShapeDtypeStruct((bsz, lp, LANES), F32)],
        scratch_shapes=[pltpu.VMEM((HEADS, 1), F32)],
        compiler_params=_params("arbitrary", "arbitrary"),
        name="gates",
    )(small3, init)


def _fox_kernel(q_ref, k_ref, v_ref, c_ref, ga_ref, o_ref, kb_ref, vb_ref, *, bq, bk, rows, scale):
    qi = pl.program_id(2)
    length = k_ref.shape[0]
    nh = kb_ref.shape[0]
    head = lambda hh: slice(hh * HEAD_DIM, (hh + 1) * HEAD_DIM)
    units = [(hh, r * rows) for hh in range(nh) for r in range(bq // rows)]

    @pl.when(qi == 0)
    def _():
        for hh in range(nh):
            kb_ref[hh] = k_ref[:, head(hh)].astype(BF16)
            vb_ref[hh, :, :HEAD_DIM] = v_ref[:, head(hh)].astype(BF16)
            vb_ref[hh, :, HEAD_DIM:] = jnp.ones((length, LANES), BF16)

    q = (q_ref[...] * (scale * LOG2E)).astype(BF16)
    qs = [q[r0:r0 + rows, head(hh)] for hh, r0 in units]

    def update(kj, carries, diag_offset=None):
        start = pl.multiple_of(kj * bk, bk)
        widths = {u: bk if diag_offset is None else min(bk, r0 + rows - diag_offset)
                  for u, (_, r0) in enumerate(units)}
        widths = {u: w for u, w in widths.items() if w > 0}
        keyed = sorted({(units[u][0], w) for u, w in widths.items()})
        kblks = {(hh, w): kb_ref[hh, pl.ds(start, w), :] for hh, w in keyed}
        cks = {(hh, w): c_ref[hh, kj, :, 0:w] * LOG2E for hh, w in keyed}
        ss = {u: lax.dot_general(qs[u], kblks[units[u][0], w], (((1,), (1,)), ((), ())),
                                 preferred_element_type=F32)
              for u, w in widths.items()}
        out = list(carries)
        for u, s in ss.items():
            m, acc, l = carries[u]
            hh, r0 = units[u]
            width = widths[u]
            lane_blocks = [slice(c * LANES, (c + 1) * LANES) for c in range(width // LANES)]
            masked = diag_offset is not None and diag_offset + width - 1 > r0
            m_parts, a_parts, p_parts = [], [], []
            for i in range(rows // FOX_STRIP):
                sl = slice(i * FOX_STRIP, (i + 1) * FOX_STRIP)
                t = s[sl] - cks[hh, width]
                if masked:
                    row = lax.broadcasted_iota(jnp.int32, (FOX_STRIP, width), 0)
                    col = lax.broadcasted_iota(jnp.int32, (FOX_STRIP, width), 1)
                    t = jnp.where(col + (diag_offset - r0 - i * FOX_STRIP) <= row, t, -jnp.inf)
                mx = t[:, lane_blocks[0]]
                for c in lane_blocks[1:]:
                    mx = jnp.maximum(mx, t[:, c])
                m_new = jnp.maximum(m[sl], jnp.broadcast_to(jnp.max(mx, axis=1, keepdims=True), mx.shape))
                m_parts.append(m_new)
                a_parts.append(jnp.exp2(m[sl] - m_new))
                p_parts.append(jnp.concatenate([jnp.exp2(t[:, c] - m_new).astype(BF16) for c in lane_blocks],
                                               axis=1))
            alpha = jnp.concatenate(a_parts, axis=0)
            pv = jnp.dot(jnp.concatenate(p_parts, axis=0), vb_ref[hh, pl.ds(start, width), :],
                         preferred_element_type=F32)
            out[u] = (jnp.concatenate(m_parts, axis=0), alpha * acc + pv[:, :HEAD_DIM],
                      alpha * l + pv[:, HEAD_DIM:])
        return tuple(out)

    init = (jnp.full((rows, LANES), -jnp.inf, F32), jnp.zeros((rows, HEAD_DIM), F32),
            jnp.zeros((rows, LANES), F32))
    per_q = bq // bk
    def past_q_block(j, cr):
        for d in range(per_q):
            cr = update(j * per_q + d, cr)
        return cr

    carries = lax.fori_loop(0, qi, past_q_block, (init,) * len(units))
    for d in range(per_q):
        carries = update(qi * per_q + d, carries, diag_offset=d * bk)
    for (hh, r0), (_, acc, l) in zip(units, carries):
        rs = slice(r0, r0 + rows)
        o_ref[rs, head(hh)] = (acc / l * _silu(ga_ref[rs, head(hh)])).astype(BF16)


def _fox_prompt(z3, k_all, v_all, layer, c_rows):
    bsz, length, _ = z3.shape
    bq = min(FOX_BQ, length)
    bk = min(FOX_BK, bq)
    rows = min(FOX_ROWS, bq)
    assert length % bq == 0 and bq % bk == 0 and bq % rows == 0
    nk = length // bk
    c5 = c_rows.reshape(bsz, HEADS, nk, 1, bk)
    nh = FOX_HEADS_PER_STEP
    hw = nh * HEAD_DIM
    assert HEADS % nh == 0 and WIDTH % hw == 0
    seg = lambda s: s * (WIDTH // hw)
    return pl.pallas_call(
        functools.partial(_fox_kernel, bq=bq, bk=bk, rows=rows, scale=HEAD_DIM ** -0.5),
        grid=(bsz, HEADS // nh, length // bq),
        in_specs=[
            pl.BlockSpec((None, bq, hw), lambda b, h, i: (b, i, seg(Z_Q) + h)),
            pl.BlockSpec((None, None, length, hw), lambda b, h, i: (layer, b, 0, h)),
            pl.BlockSpec((None, None, length, hw), lambda b, h, i: (layer, b, 0, h)),
            pl.BlockSpec((None, nh, nk, 1, bk), lambda b, h, i: (b, h, 0, 0, 0)),
            pl.BlockSpec((None, bq, hw), lambda b, h, i: (b, i, seg(Z_GATE_A) + h)),
        ],
        out_specs=pl.BlockSpec((None, bq, hw), lambda b, h, i: (b, i, h)),
        out_shape=jax.ShapeDtypeStruct((bsz, length, WIDTH), BF16),
        scratch_shapes=[pltpu.VMEM((nh, length, HEAD_DIM), BF16),
                        pltpu.VMEM((nh, length, 2 * HEAD_DIM), BF16)],
        compiler_params=_params("arbitrary", "arbitrary", "arbitrary"),
        name="fox_prompt",
    )(z3, k_all, v_all, c5, z3)


def _fox_decode_kernel(q_ref, kn_ref, vn_ref, cn_ref, ga_ref, kp_ref, vp_ref, cp_ref, o_ref,
                       m_ref, acc_ref, l_ref, *, n, scale):
    j = pl.program_id(1)
    nrow = q_ref.shape[0]
    q = (q_ref[...] * (scale * LOG2E)).astype(BF16)

    @pl.when(j == 0)
    def _():
        m_ref[...] = jnp.full(m_ref.shape, -jnp.inf, F32)
        acc_ref[...] = jnp.zeros(acc_ref.shape, F32)
        l_ref[...] = jnp.zeros(l_ref.shape, F32)

    def update(kblk, vblk, c_row, visible):
        t = jnp.where(visible, _mm_nt(q, kblk) - c_row * LOG2E, -jnp.inf)
        m = m_ref[...]
        m_new = jnp.maximum(m, jnp.broadcast_to(jnp.max(t, axis=1, keepdims=True), m.shape))
        alpha = jnp.exp2(m - m_new)
        p = jnp.concatenate([jnp.exp2(t[:, c * LANES:(c + 1) * LANES] - m_new)
                             for c in range(t.shape[1] // LANES)], axis=1)
        m_ref[...] = m_new
        l_ref[...] = alpha * l_ref[...] + jnp.broadcast_to(jnp.sum(p, axis=1, keepdims=True), m.shape)
        acc_ref[...] = alpha * acc_ref[...] + _mm(p, vblk)

    cols = kp_ref.shape[0]
    row_head = lax.broadcasted_iota(jnp.int32, (nrow, cols), 0) // n
    col_head = lax.broadcasted_iota(jnp.int32, (nrow, cols), 1) % HEADS
    update(kp_ref[...], vp_ref[...], cp_ref[...], row_head == col_head)

    @pl.when(j == pl.num_programs(1) - 1)
    def _():
        r = lax.broadcasted_iota(jnp.int32, (nrow, nrow), 0)
        c = lax.broadcasted_iota(jnp.int32, (nrow, nrow), 1)
        update(kn_ref[...], vn_ref[...], cn_ref[...], (r // n == c // n) & (c % n <= r % n))
        o_ref[...] = (acc_ref[...] / l_ref[...] * _silu(ga_ref[...])).astype(BF16)


def _fox_decode(z3, k4, v4, n_tok, cache_k, cache_v, layer, cp_flat, cn_rows):
    bsz = z3.shape[0]
    past = cache_k.shape[2]
    nrow = HEADS * n_tok
    pc = min(1024, past)
    assert past % pc == 0 and nrow % LANES == 0

    def head_rows(a):
        return a.reshape(bsz, n_tok, HEADS, HEAD_DIM).transpose(0, 2, 1, 3).reshape(bsz, nrow, HEAD_DIM)

    seg = lambda s: z3[:, :n_tok, s * WIDTH:(s + 1) * WIDTH]
    k2 = cache_k.reshape(cache_k.shape[:2] + (past * HEADS, HEAD_DIM))
    v2 = cache_v.reshape(cache_v.shape[:2] + (past * HEADS, HEAD_DIM))
    rows_blk = pl.BlockSpec((None, nrow, HEAD_DIM), lambda b, j: (b, 0, 0))
    cache_blk = pl.BlockSpec((None, None, pc * HEADS, HEAD_DIM), lambda b, j: (layer, b, j, 0))
    o_rows = pl.pallas_call(
        functools.partial(_fox_decode_kernel, n=n_tok, scale=HEAD_DIM ** -0.5),
        grid=(bsz, past // pc),
        in_specs=[
            rows_blk, rows_blk, rows_blk,
            pl.BlockSpec((None, 1, nrow), lambda b, j: (b, 0, 0)),
            rows_blk, cache_blk, cache_blk,
            pl.BlockSpec((None, None, 1, pc * HEADS), lambda b, j: (layer, b, 0, j)),
        ],
        out_specs=rows_blk,
        out_shape=jax.ShapeDtypeStruct((bsz, nrow, HEAD_DIM), BF16),
        scratch_shapes=[pltpu.VMEM((nrow, LANES), F32), pltpu.VMEM((nrow, HEAD_DIM), F32),
                        pltpu.VMEM((nrow, LANES), F32)],
        compiler_params=_params("arbitrary", "arbitrary"),
        name="fox_decode",
    )(head_rows(seg(Z_Q)), head_rows(k4[layer, :, :n_tok]), head_rows(v4[layer, :, :n_tok]),
      cn_rows.reshape(bsz, 1, nrow), head_rows(seg(Z_GATE_A)), k2, v2, cp_flat)
    return o_rows.reshape(bsz, HEADS, n_tok, HEAD_DIM).transpose(0, 2, 1, 3).reshape(bsz, n_tok, WIDTH)


def _gdn_kernel(qp_ref, kp_ref, vp_ref, cw_ref, c0_ref, cols_ref, gr_ref, gb_ref, nw_ref, s0_ref,
                og_ref, s_ref, xp_ref, *, chunk, scale):
    pad = SUBLANES
    nb = qp_ref.shape[0]

    @pl.when(pl.program_id(1) == 0)
    def _():
        s_ref[...] = s0_ref[...]
        xp_ref[:, 0:pad, :] = c0_ref[...]

    convs = []
    for bb in range(nb):
        xp_ref[bb, pad:pad + chunk, 0:WIDTH] = qp_ref[bb]
        xp_ref[bb, pad:pad + chunk, WIDTH:2 * WIDTH] = kp_ref[bb]
        xp_ref[bb, pad:pad + chunk, 2 * WIDTH:3 * WIDTH] = vp_ref[bb]
        xp = xp_ref[bb]
        acc = xp * cw_ref[0:1, :]
        for i in range(1, CONV_W):
            acc = pltpu.roll(acc, 1, axis=0) + xp * cw_ref[i:i + 1, :]
        convs.append(_silu(acc[pad:pad + chunk]))
        xp_ref[bb, 0:pad, :] = xp_ref[bb, chunk:chunk + pad, :]

    row = lax.broadcasted_iota(jnp.int32, (chunk, chunk), 0)
    col = lax.broadcasted_iota(jnp.int32, (chunk, chunk), 1)
    incl = row >= col
    strict = row > col
    sizes = [INV_BASE << i for i in range(int(math.log2(chunk // INV_BASE)) + 1)]
    same = [(row >> int(math.log2(b))) == (col >> int(math.log2(b))) for b in sizes[:-1]]
    diag_mask = strict & same[0]
    off_masks = [strict & ~same[i] & (same[i + 1] if i + 1 < len(same) else True)
                 for i in range(len(same))]

    units = [(bb, hh) for bb in range(nb) for hh in range(HEADS)]
    heads = range(len(units))
    sl = lambda seg, hh: slice(seg * WIDTH + hh * HEAD_DIM, seg * WIDTH + (hh + 1) * HEAD_DIM)
    q, k, v, gc, beta, e, lmat, qk = ([None] * len(units) for _ in range(8))
    for h, (bb, hh) in enumerate(units):
        qh, kh = convs[bb][:, sl(0, hh)], convs[bb][:, sl(1, hh)]
        q[h] = qh * (lax.rsqrt(jnp.sum(qh * qh, axis=1, keepdims=True) + RMS_EPS) * scale)
        k[h] = kh * lax.rsqrt(jnp.sum(kh * kh, axis=1, keepdims=True) + RMS_EPS)
        v[h] = convs[bb][:, sl(2, hh)]
        gc[h] = cols_ref[bb, :, hh:hh + 1]
        beta[h] = cols_ref[bb, :, HEADS + hh:HEADS + hh + 1]
        e[h] = jnp.exp(gc[h])
    for h, (bb, hh) in enumerate(units):
        decay = jnp.exp(jnp.where(incl, gc[h] - gr_ref[bb, hh:hh + 1, :], -jnp.inf))
        kq = _mm_nt(jnp.concatenate([k[h] * beta[h], q[h]], axis=0), k[h])
        lmat[h] = jnp.where(strict, kq[:chunk] * decay, 0.0)
        qk[h] = jnp.where(incl, kq[chunk:] * decay, 0.0)
    xpow = [jnp.where(diag_mask, lmat[h], 0.0) for h in heads]
    nm = [-xpow[h] for h in heads]
    for _ in range(int(math.log2(INV_BASE)) - 1):
        xpow = [_mm(xpow[h], xpow[h]) for h in heads]
        nm = [nm[h] + xpow[h] + _mm(nm[h], xpow[h]) for h in heads]
    for off in off_masks:
        loff = [jnp.where(off, lmat[h], 0.0) for h in heads]
        t_loff = [loff[h] + _mm(nm[h], loff[h]) for h in heads]
        nm = [nm[h] - (t_loff[h] + _mm(t_loff[h], nm[h])) for h in heads]
    uw = []
    for h in heads:
        rhs = jnp.concatenate([v[h] * beta[h], k[h] * (beta[h] * e[h])], axis=1)
        uw.append(rhs + _mm(nm[h], rhs))
    s_old = [s_ref[bb, hh] for bb, hh in units]
    ws_qs = [_mm(jnp.concatenate([uw[h][:, HEAD_DIM:], q[h] * e[h]], axis=0), s_old[h]) for h in heads]
    v_new = [uw[h][:, :HEAD_DIM] - ws_qs[h][:chunk] for h in heads]
    for h, (bb, hh) in enumerate(units):
        g_last = gc[h][chunk - 1:chunk, :]
        k_tail = k[h] * jnp.exp(g_last - gc[h])
        s_ref[bb, hh] = s_old[h] * jnp.exp(g_last) + _mm_tn(k_tail, v_new[h])
    for h, (bb, hh) in enumerate(units):
        o = ws_qs[h][chunk:] + _mm(qk[h], v_new[h])
        o = o * lax.rsqrt(jnp.mean(o * o, axis=1, keepdims=True) + RMS_EPS) * nw_ref[...]
        og_ref[bb, :, sl(0, hh)] = (o * _silu(gb_ref[bb, :, sl(0, hh)])).astype(BF16)


def _gdn(z3, seg_qkv, seg_gate, conv_w, conv0, cols, gc_rows, norm_w, s0):
    bsz, length, _ = z3.shape
    chunk = GDN_CHUNK
    assert length % chunk == 0
    nc = length // chunk
    nb = GDN_BATCHES if bsz % GDN_BATCHES == 0 else 1
    c0 = jnp.pad(conv0, ((0, 0), (SUBLANES - (CONV_W - 1), 0), (0, 0)))
    zblk = lambda seg: pl.BlockSpec((nb, chunk, WIDTH), lambda b, l: (b, l, seg))
    return pl.pallas_call(
        functools.partial(_gdn_kernel, chunk=chunk, scale=HEAD_DIM ** -0.5),
        grid=(bsz // nb, nc),
        in_specs=[
            zblk(seg_qkv), zblk(seg_qkv + 1), zblk(seg_qkv + 2),
            pl.BlockSpec((CONV_W, CONV_CH), lambda b, l: (0, 0)),
            pl.BlockSpec((nb, SUBLANES, CONV_CH), lambda b, l: (b, 0, 0)),
            pl.BlockSpec((nb, chunk, LANES), lambda b, l: (b, l, 0)),
            pl.BlockSpec((nb, HEADS, chunk), lambda b, l: (b, 0, l)),
            zblk(seg_gate),
            pl.BlockSpec((1, HEAD_DIM), lambda b, l: (0, 0)),
            pl.BlockSpec((nb, HEADS, HEAD_DIM, HEAD_DIM), lambda b, l: (b, 0, 0, 0)),
        ],
        out_specs=[
            pl.BlockSpec((nb, chunk, WIDTH), lambda b, l: (b, l, 0)),
            pl.BlockSpec((nb, HEADS, HEAD_DIM, HEAD_DIM), lambda b, l: (b, 0, 0, 0)),
        ],
        out_shape=[
            jax.ShapeDtypeStruct((bsz, length, WIDTH), BF16),
            jax.ShapeDtypeStruct((bsz, HEADS, HEAD_DIM, HEAD_DIM), F32),
        ],
        scratch_shapes=[pltpu.VMEM((nb, chunk + SUBLANES, CONV_CH), F32)],
        compiler_params=_params("arbitrary", "arbitrary"),
        name="gdn",
    )(z3, z3, z3, conv_w, c0, cols, gc_rows, z3, norm_w.reshape(1, HEAD_DIM), s0)


def _out_kernel(oa_ref, ob_ref, ma_ref, mb_ref, x_ref, p_ref, wfox_ref, wgdn_ref, wout_ref,
                wplp_ref, wplg_ref, vec_ref, o_ref, *, alpha):
    ya = jnp.dot(oa_ref[...], wfox_ref[...], preferred_element_type=F32)
    yb = jnp.dot(ob_ref[...], wgdn_ref[...], preferred_element_type=F32)
    merged = _sigmoid(ma_ref[...]) * ya + _sigmoid(mb_ref[...]) * yb
    y = alpha * x_ref[...] + _mm(merged, wout_ref[...])
    yc = y - jnp.mean(y, axis=1, keepdims=True)
    var = jnp.mean(yc * yc, axis=1, keepdims=True)
    x1 = yc * lax.rsqrt(var + LN_EPS) * vec_ref[0:1, :] + vec_ref[1:2, :]
    e = _mm(p_ref[...], wplp_ref[...])
    e = e * lax.rsqrt(jnp.mean(e * e, axis=1, keepdims=True) + RMS_EPS) * vec_ref[2:3, :]
    o_ref[...] = x1 + _sigmoid(_mm(x1, wplg_ref[...])) * e


def _out_block(oa, ob, z2d, x2d, p_all, layer, wfox, wgdn, wout, wplp, wplg, vec, alpha):
    t_rows, d = x2d.shape
    ple = p_all.shape[-1]
    tm = min(256, t_rows)
    dpw = d // WIDTH
    assert d % WIDTH == 0 and Z_M_A % dpw == 0 and Z_M_B % dpw == 0
    const = lambda shape: pl.BlockSpec((None,) + shape, lambda r: (layer,) + (0,) * len(shape),
                                       pipeline_mode=pl.Buffered(1))
    return pl.pallas_call(
        functools.partial(_out_kernel, alpha=alpha),
        grid=(t_rows // tm,),
        in_specs=[
            pl.BlockSpec((tm, WIDTH), lambda r: (r, 0)),
            pl.BlockSpec((tm, WIDTH), lambda r: (r, 0)),
            pl.BlockSpec((tm, d), lambda r: (r, Z_M_A // dpw)),
            pl.BlockSpec((tm, d), lambda r: (r, Z_M_B // dpw)),
            pl.BlockSpec((tm, d), lambda r: (r, 0)),
            pl.BlockSpec((None, tm, ple), lambda r: (layer, r, 0)),
            const((WIDTH, d)), const((WIDTH, d)), const((d, d)), const((ple, d)), const((d, d)),
            const((SUBLANES, d)),
        ],
        out_specs=pl.BlockSpec((tm, d), lambda r: (r, 0)),
        out_shape=jax.ShapeDtypeStruct((t_rows, d), F32),
        compiler_params=_params("arbitrary"),
        name="out_block",
    )(oa, ob, z2d, z2d, x2d, p_all, wfox, wgdn, wout, wplp, wplg, vec)


def _layer(x2d, bsz, n_tok, p_all, layer, wts, conv0, s0, alpha, k_all, v_all, fox_cache=None):
    lp = n_tok
    z2d, small, k_all, v_all = _proj(x2d, wts["w_big"], wts["w_small"], wts["par"], k_all, v_all, layer)
    z3 = z2d.reshape(bsz, lp, -1)
    k4 = k_all.reshape(-1, bsz, lp, WIDTH)
    v4 = v_all.reshape(-1, bsz, lp, WIDTH)
    small3 = small.reshape(bsz, lp, LANES)
    lg = -(-lp // GDN_CHUNK) * GDN_CHUNK
    if lg == lp:
        zg, seg_qkv, seg_gate = z3, Z_QKV_B, Z_GATE_B
    else:
        pad_rows = lambda a: jnp.pad(a, ((0, 0), (0, lg - lp), (0, 0)))
        zg = pad_rows(z3[:, :, Z_QKV_B * WIDTH:(Z_GATE_B + 1) * WIDTH])
        seg_qkv, seg_gate = 0, Z_GATE_B - Z_QKV_B
        small3 = pad_rows(small3)
    c_init = (jnp.zeros((bsz, HEADS, 1), F32) if fox_cache is None
              else fox_cache[3][layer].reshape(bsz, HEADS, 1))
    logf_rows, c_rows, gc_rows, cols = _gates(small3, n_tok, c_init)
    logf = logf_rows[:, :, :n_tok].transpose(0, 2, 1)

    if fox_cache is None:
        oa = _fox_prompt(z3, k4, v4, layer, c_rows)
    else:
        cache_k, cache_v, cp_flat, _ = fox_cache
        oa = _fox_decode(z3, k4, v4, n_tok, cache_k, cache_v, layer, cp_flat, c_rows[:, :, :n_tok])
    ob, s_new = _gdn(zg, seg_qkv, seg_gate, wts["conv_w"], conv0, cols, gc_rows, wts["norm_w"], s0)
    ob = ob[:, :lp]

    x_out = _out_block(oa.reshape(bsz * lp, WIDTH), ob.reshape(bsz * lp, WIDTH), z2d, x2d, p_all, layer,
                       wts["wfox"], wts["wgdn"], wts["wout"], wts["wplp"], wts["wplg"], wts["vec"], alpha)

    new_conv = z3[:, n_tok - (CONV_W - 1):n_tok, Z_QKV_B * WIDTH:Z_QKV_B * WIDTH + CONV_CH]
    return x_out, k_all, v_all, logf, new_conv, s_new


def kernel(x_prompt, x_sample, cache_fox_k, cache_fox_v, cache_fox_logf, state_gdn_conv, state_gdn, p_prompt, p_sample, w_in, fox_f_bias, gdn_conv_w, gdn_a_log, gdn_dt_bias, gdn_norm_w, w_out_fox, w_out_gdn, w_out, ln_g, ln_b, w_pl_proj, w_pl_gate, pl_norm_w):
    depth = w_in.shape[0]
    bp, seq, d = x_prompt.shape
    bs, dseq, _ = x_sample.shape
    past = cache_fox_k.shape[2]
    ple = p_prompt.shape[-1]
    assert d % WIDTH == 0 and dseq >= CONV_W - 1 and seq >= CONV_W - 1
    assert dseq % SUBLANES == 0 and seq % SUBLANES == 0
    alpha = (2 * depth) ** 0.25

    o_f = 3 * WIDTH
    o_ga = o_f + HEADS
    o_qkv = o_ga + WIDTH
    o_a = o_qkv + CONV_CH
    o_gb = o_a + 2 * HEADS
    w_in_t = jnp.swapaxes(w_in, 1, 2).astype(BF16)
    w_big = jnp.concatenate([w_in_t[:, :WIDTH], w_in_t[:, o_ga:o_a], w_in_t[:, o_gb:],
                             w_in_t[:, WIDTH:o_f]], axis=1)
    w_small = jnp.concatenate([w_in[..., o_f:o_ga], w_in[..., o_a:o_gb],
                               jnp.zeros((depth, d, LANES - 3 * HEADS), F32)], axis=-1)
    w_small_hi = w_small.astype(BF16)
    w_small = jnp.stack([w_small_hi, (w_small - w_small_hi.astype(F32)).astype(BF16)], axis=1)
    zpad = jnp.zeros((depth, LANES - 2 * HEADS), F32)
    par = jnp.stack([jnp.concatenate([fox_f_bias, gdn_dt_bias, zpad], axis=-1),
                     jnp.concatenate([jnp.zeros_like(fox_f_bias), gdn_a_log, zpad], axis=-1)], axis=1)
    par = jnp.pad(par, ((0, 0), (0, SUBLANES - 2), (0, 0)))
    vec = jnp.pad(jnp.stack([ln_g, ln_b, pl_norm_w], axis=1), ((0, 0), (0, SUBLANES - 3), (0, 0)))
    wfox, wgdn, wout = w_out_fox.astype(BF16), w_out_gdn.astype(BF16), w_out.astype(BF16)
    wplp, wplg = w_pl_proj.astype(BF16), w_pl_gate.astype(BF16)

    def layer_weights(i):
        return dict(w_big=w_big, w_small=w_small, par=par, conv_w=gdn_conv_w[i],
                    norm_w=gdn_norm_w[i], wfox=wfox, wgdn=wgdn, wout=wout,
                    wplp=wplp, wplg=wplg, vec=vec)

    xs = x_sample.reshape(bs * dseq, d)
    ps_all = p_sample.reshape(depth, bs * dseq, ple)
    xp = x_prompt.reshape(bp * seq, d)
    pp_all = p_prompt.reshape(depth, bp * seq, ple)

    cp_rows = _cumsum_rows(cache_fox_logf.transpose(0, 1, 3, 2).reshape(depth * bs * HEADS, past))
    cp_rows = cp_rows.reshape(depth, bs, HEADS, past)
    cp_flat = cp_rows.transpose(0, 1, 3, 2).reshape(depth, bs, 1, past * HEADS)
    cp_last = cp_rows[..., past - 1:].reshape(depth, bs * HEADS, 1)

    conv0_p = jnp.zeros((bp, CONV_W - 1, CONV_CH), F32)
    s0_p = jnp.zeros((bp, HEADS, HEAD_DIM, HEAD_DIM), F32)
    kp = lax.empty((depth, bp * seq, WIDTH), F32)
    vp = lax.empty((depth, bp * seq, WIDTH), F32)
    ks = lax.empty((depth, bs * dseq, WIDTH), F32)
    vs = lax.empty((depth, bs * dseq, WIDTH), F32)
    outs_p, outs_s = [], []
    for i in range(depth):
        wts = layer_weights(i)
        xp, kp, vp, *rest_p = _layer(xp, bp, seq, pp_all, i, wts, conv0_p, s0_p, alpha, kp, vp)
        xs, ks, vs, *rest_s = _layer(xs, bs, dseq, ps_all, i, wts, state_gdn_conv[i], state_gdn[i], alpha,
                                     ks, vs, fox_cache=(cache_fox_k, cache_fox_v, cp_flat, cp_last))
        outs_p.append(rest_p)
        outs_s.append(rest_s)
    stack = lambda outs, j: jnp.stack([o[j] for o in outs])
    heads_p = lambda a: a.reshape(depth, bp, seq, HEADS, HEAD_DIM)
    heads_s = lambda a: a.reshape(depth, bs, dseq, HEADS, HEAD_DIM)
    y_prompt = xp.reshape(bp, seq, d)
    y_sample = xs.reshape(bs, dseq, d)
    return (y_prompt, y_sample,
            heads_p(kp), heads_p(vp), stack(outs_p, 0), stack(outs_p, 1), stack(outs_p, 2),
            heads_s(ks), heads_s(vs), stack(outs_s, 0), stack(outs_s, 1), stack(outs_s, 2))
```

```python
import functools
import math

import jax
import jax.numpy as jnp
from jax import lax
from jax.experimental import pallas as pl
from jax.experimental.pallas import tpu as pltpu

F32 = jnp.float32
BF16 = jnp.bfloat16

HEADS = 8
HEAD_DIM = 128
WIDTH = HEADS * HEAD_DIM
CONV_W = 4
CONV_CH = 3 * WIDTH
LN_EPS = 1e-5
RMS_EPS = 1e-6
LOG2E = math.log2(math.e)
LANES = 128
SUBLANES = 8
GDN_CHUNK = 128
GDN_BATCHES = 2
INV_BASE = 8
FOX_BQ = 2048
FOX_BK = 1024
FOX_ROWS = 256
FOX_HEADS_PER_STEP = 1
FOX_STRIP = 16
VMEM_LIMIT = 56 * 1024 * 1024

Z_Q, Z_GATE_A, Z_QKV_B, Z_GATE_B, Z_M_A, Z_M_B, Z_END = 0, 1, 2, 5, 6, 8, 10


def _params(*sem):
    return pltpu.CompilerParams(dimension_semantics=sem, vmem_limit_bytes=VMEM_LIMIT)


def _mm(a, b):
    return jnp.dot(a.astype(BF16), b.astype(BF16), preferred_element_type=F32)


def _mm_nt(a, b):
    return lax.dot_general(a.astype(BF16), b.astype(BF16), (((1,), (1,)), ((), ())),
                           preferred_element_type=F32)


def _mm_tn(a, b):
    return lax.dot_general(a.astype(BF16), b.astype(BF16), (((0,), (0,)), ((), ())),
                           preferred_element_type=F32)


def _sigmoid(x):
    return 0.5 * jnp.tanh(0.5 * x) + 0.5


def _silu(x):
    return x * _sigmoid(x)


def _proj_kernel(x_ref, w_ref, ws_ref, par_ref, k_in_ref, v_in_ref, z_ref, s_ref, k_ref, v_ref, xb_ref,
                 *, z_tiles, kv_tiles):
    del k_in_ref, v_in_ref
    j = pl.program_id(1)

    @pl.when(j == 0)
    def _():
        x = x_ref[...]
        xb = x.astype(BF16)
        xb_ref[...] = xb
        x_lo = (x - xb.astype(F32)).astype(BF16)
        raw = (jnp.dot(xb, ws_ref[0], preferred_element_type=F32)
               + jnp.dot(x_lo, ws_ref[0], preferred_element_type=F32)
               + jnp.dot(xb, ws_ref[1], preferred_element_type=F32))
        y = raw + par_ref[0:1, :]
        neg_a = -jnp.exp(par_ref[1:2, :])
        t = jnp.log1p(jnp.exp(-jnp.abs(y)))
        logf = jnp.minimum(y, 0.0) - t
        g = neg_a * (jnp.maximum(y, 0.0) + t)
        beta = _sigmoid(raw)
        lane = lax.broadcasted_iota(jnp.int32, raw.shape, 1)
        s_ref[...] = jnp.where(lane < HEADS, logf,
                               jnp.where(lane < 2 * HEADS, g,
                                         jnp.where(lane < 3 * HEADS, beta, 0.0)))

    @pl.when(j < z_tiles)
    def _():
        z_ref[...] = _mm_nt(xb_ref[...], w_ref[...])

    @pl.when((j >= z_tiles) & (j < z_tiles + kv_tiles))
    def _():
        k_ref[...] = _mm_nt(xb_ref[...], w_ref[...])

    @pl.when(j >= z_tiles + kv_tiles)
    def _():
        v_ref[...] = _mm_nt(xb_ref[...], w_ref[...])


def _proj(x2d, w_big, w_small, par, k_all, v_all, layer):
    t_rows, d = x2d.shape
    tm = min(1024, t_rows)
    tn = 512
    kv_tiles = WIDTH // tn
    z_tiles = w_big.shape[1] // tn - 2 * kv_tiles
    clamp = lambda j, lo: jnp.clip(j - lo, 0, kv_tiles - 1)
    return pl.pallas_call(
        functools.partial(_proj_kernel, z_tiles=z_tiles, kv_tiles=kv_tiles),
        grid=(t_rows // tm, z_tiles + 2 * kv_tiles),
        in_specs=[
            pl.BlockSpec((tm, d), lambda i, j: (i, 0)),
            pl.BlockSpec((None, tn, d), lambda i, j: (layer, j, 0)),
            pl.BlockSpec((None, 2, d, LANES), lambda i, j: (layer, 0, 0, 0)),
            pl.BlockSpec((None, SUBLANES, LANES), lambda i, j: (layer, 0, 0)),
            pl.BlockSpec(memory_space=pl.ANY),
            pl.BlockSpec(memory_space=pl.ANY),
        ],
        out_specs=[
            pl.BlockSpec((tm, tn), lambda i, j: (i, jnp.minimum(j, z_tiles - 1))),
            pl.BlockSpec((tm, LANES), lambda i, j: (i, 0)),
            pl.BlockSpec((None, tm, tn), lambda i, j: (layer, i, clamp(j, z_tiles))),
            pl.BlockSpec((None, tm, tn), lambda i, j: (layer, i, clamp(j, z_tiles + kv_tiles))),
        ],
        out_shape=[
            jax.ShapeDtypeStruct((t_rows, z_tiles * tn), F32),
            jax.ShapeDtypeStruct((t_rows, LANES), F32),
            jax.ShapeDtypeStruct(k_all.shape, F32),
            jax.ShapeDtypeStruct(v_all.shape, F32),
        ],
        input_output_aliases={4: 2, 5: 3},
        scratch_shapes=[pltpu.VMEM((tm, d), BF16)],
        compiler_params=_params("arbitrary", "arbitrary"),
        name="proj",
    )(x2d, w_big, w_small, par, k_all, v_all)


def _cumsum_kernel(x_ref, o_ref, carry_ref, *, blk):
    r = lax.broadcasted_iota(jnp.int32, (blk, blk), 0)
    c = lax.broadcasted_iota(jnp.int32, (blk, blk), 1)
    y = jnp.dot(x_ref[...], (r <= c).astype(F32), precision=lax.Precision.HIGHEST,
                preferred_element_type=F32)

    @pl.when(pl.program_id(0) == 0)
    def _():
        carry_ref[...] = jnp.zeros_like(carry_ref)

    y = y + carry_ref[...]
    carry_ref[...] = y[:, blk - 1:blk]
    o_ref[...] = y


def _cumsum_rows(x):
    rows, length = x.shape
    blk = min(512, length)
    assert length % blk == 0
    return pl.pallas_call(
        functools.partial(_cumsum_kernel, blk=blk),
        grid=(length // blk,),
        in_specs=[pl.BlockSpec((rows, blk), lambda j: (0, j))],
        out_specs=pl.BlockSpec((rows, blk), lambda j: (0, j)),
        out_shape=jax.ShapeDtypeStruct((rows, length), F32),
        scratch_shapes=[pltpu.VMEM((rows, 1), F32)],
        compiler_params=_params("arbitrary"),
        name="cumsum",
    )(x)


def _gates_kernel(s_ref, init_ref, lf_ref, c_ref, gc_ref, cols_ref, carry_ref, *, blk, period, n_tok):
    j = pl.program_id(1)

    @pl.when(j == 0)
    def _():
        carry_ref[...] = init_ref[...]

    x = s_ref[...]
    pos = j * blk + lax.broadcasted_iota(jnp.int32, x.shape, 0)
    lane = lax.broadcasted_iota(jnp.int32, x.shape, 1)
    x = jnp.where((pos >= n_tok) & (lane >= HEADS), 0.0, x)
    xt = x.T
    r = lax.broadcasted_iota(jnp.int32, (blk, blk), 0)
    c = lax.broadcasted_iota(jnp.int32, (blk, blk), 1)
    upper = r <= c
    local = upper & ((r // period) == (c // period))
    rows = xt[0:2 * HEADS]
    run = jnp.dot(rows, upper.astype(F32), precision=lax.Precision.HIGHEST, preferred_element_type=F32)
    loc = jnp.dot(rows, local.astype(F32), precision=lax.Precision.HIGHEST, preferred_element_type=F32)
    csum = run[0:HEADS] + carry_ref[...]
    carry_ref[...] = csum[:, blk - 1:blk]
    gc = loc[HEADS:2 * HEADS]
    lf_ref[...] = xt[0:HEADS]
    c_ref[...] = csum
    gc_ref[...] = gc
    cols_t = jnp.concatenate([gc, xt[2 * HEADS:3 * HEADS], jnp.zeros((LANES - 2 * HEADS, blk), F32)], axis=0)
    cols_ref[...] = cols_t.T


def _gates(small3, n_tok, init):
    bsz, lp, _ = small3.shape
    blk = min(512, lp)
    assert lp % blk == 0 and blk % GDN_CHUNK == 0
    rows_blk = pl.BlockSpec((None, HEADS, blk), lambda b, j: (b, 0, j))
    return pl.pallas_call(
        functools.partial(_gates_kernel, blk=blk, period=GDN_CHUNK, n_tok=n_tok),
        grid=(bsz, lp // blk),
        in_specs=[pl.BlockSpec((None, blk, LANES), lambda b, j: (b, j, 0)),
                  pl.BlockSpec((None, HEADS, 1), lambda b, j: (b, 0, 0))],
        out_specs=[rows_blk, rows_blk, rows_blk, pl.BlockSpec((None, blk, LANES), lambda b, j: (b, j, 0))],
        out_shape=[jax.ShapeDtypeStruct((bsz, HEADS, lp), F32)] * 3
                  + [jax.ShapeDtypeStruct((bsz, lp, LANES), F32)],
        scratch_shapes=[pltpu.VMEM((HEADS, 1), F32)],
        compiler_params=_params("arbitrary", "arbitrary"),
        name="gates",
    )(small3, init)


def _fox_kernel(q_ref, k_ref, v_ref, c_ref, ga_ref, o_ref, kb_ref, vb_ref, *, bq, bk, rows, scale):
    qi = pl.program_id(2)
    length = k_ref.shape[0]
    nh = kb_ref.shape[0]
    head = lambda hh: slice(hh * HEAD_DIM, (hh + 1) * HEAD_DIM)
    units = [(hh, r * rows) for hh in range(nh) for r in range(bq // rows)]

    @pl.when(qi == 0)
    def _():
        for hh in range(nh):
            kb_ref[hh] = k_ref[:, head(hh)].astype(BF16)
            vb_ref[hh, :, :HEAD_DIM] = v_ref[:, head(hh)].astype(BF16)
            vb_ref[hh, :, HEAD_DIM:] = jnp.ones((length, LANES), BF16)

    q = (q_ref[...] * (scale * LOG2E)).astype(BF16)
    qs = [q[r0:r0 + rows, head(hh)] for hh, r0 in units]

    def update(kj, carries, diag_offset=None):
        start = pl.multiple_of(kj * bk, bk)
        widths = {u: bk if diag_offset is None else min(bk, r0 + rows - diag_offset)
                  for u, (_, r0) in enumerate(units)}
        widths = {u: w for u, w in widths.items() if w > 0}
        keyed = sorted({(units[u][0], w) for u, w in widths.items()})
        kblks = {(hh, w): kb_ref[hh, pl.ds(start, w), :] for hh, w in keyed}
        cks = {(hh, w): c_ref[hh, kj, :, 0:w] * LOG2E for hh, w in keyed}
        ss = {u: lax.dot_general(qs[u], kblks[units[u][0], w], (((1,), (1,)), ((), ())),
                                 preferred_element_type=F32)
              for u, w in widths.items()}
        out = list(carries)
        for u, s in ss.items():
            m, acc, l = carries[u]
            hh, r0 = units[u]
            width = widths[u]
            lane_blocks = [slice(c * LANES, (c + 1) * LANES) for c in range(width // LANES)]
            masked = diag_offset is not None and diag_offset + width - 1 > r0
            m_parts, a_parts, p_parts = [], [], []
            for i in range(rows // FOX_STRIP):
                sl = slice(i * FOX_STRIP, (i + 1) * FOX_STRIP)
                t = s[sl] - cks[hh, width]
                if masked:
                    row = lax.broadcasted_iota(jnp.int32, (FOX_STRIP, width), 0)
                    col = lax.broadcasted_iota(jnp.int32, (FOX_STRIP, width), 1)
                    t = jnp.where(col + (diag_offset - r0 - i * FOX_STRIP) <= row, t, -jnp.inf)
                mx = t[:, lane_blocks[0]]
                for c in lane_blocks[1:]:
                    mx = jnp.maximum(mx, t[:, c])
                m_new = jnp.maximum(m[sl], jnp.broadcast_to(jnp.max(mx, axis=1, keepdims=True), mx.shape))
                m_parts.append(m_new)
                a_parts.append(jnp.exp2(m[sl] - m_new))
                p_parts.append(jnp.concatenate([jnp.exp2(t[:, c] - m_new).astype(BF16) for c in lane_blocks],
                                               axis=1))
            alpha = jnp.concatenate(a_parts, axis=0)
            pv = jnp.dot(jnp.concatenate(p_parts, axis=0), vb_ref[hh, pl.ds(start, width), :],
                         preferred_element_type=F32)
            out[u] = (jnp.concatenate(m_parts, axis=0), alpha * acc + pv[:, :HEAD_DIM],
                      alpha * l + pv[:, HEAD_DIM:])
        return tuple(out)

    init = (jnp.full((rows, LANES), -jnp.inf, F32), jnp.zeros((rows, HEAD_DIM), F32),
            jnp.zeros((rows, LANES), F32))
    per_q = bq // bk
    def past_q_block(j, cr):
        for d in range(per_q):
            cr = update(j * per_q + d, cr)
        return cr

    carries = lax.fori_loop(0, qi, past_q_block, (init,) * len(units))
    for d in range(per_q):
        carries = update(qi * per_q + d, carries, diag_offset=d * bk)
    for (hh, r0), (_, acc, l) in zip(units, carries):
        rs = slice(r0, r0 + rows)
        o_ref[rs, head(hh)] = (acc / l * _silu(ga_ref[rs, head(hh)])).astype(BF16)


def _fox_prompt(z3, k_all, v_all, layer, c_rows):
    bsz, length, _ = z3.shape
    bq = min(FOX_BQ, length)
    bk = min(FOX_BK, bq)
    rows = min(FOX_ROWS, bq)
    assert length % bq == 0 and bq % bk == 0 and bq % rows == 0
    nk = length // bk
    c5 = c_rows.reshape(bsz, HEADS, nk, 1, bk)
    nh = FOX_HEADS_PER_STEP
    hw = nh * HEAD_DIM
    assert HEADS % nh == 0 and WIDTH % hw == 0
    seg = lambda s: s * (WIDTH // hw)
    return pl.pallas_call(
        functools.partial(_fox_kernel, bq=bq, bk=bk, rows=rows, scale=HEAD_DIM ** -0.5),
        grid=(bsz, HEADS // nh, length // bq),
        in_specs=[
            pl.BlockSpec((None, bq, hw), lambda b, h, i: (b, i, seg(Z_Q) + h)),
            pl.BlockSpec((None, None, length, hw), lambda b, h, i: (layer, b, 0, h)),
            pl.BlockSpec((None, None, length, hw), lambda b, h, i: (layer, b, 0, h)),
            pl.BlockSpec((None, nh, nk, 1, bk), lambda b, h, i: (b, h, 0, 0, 0)),
            pl.BlockSpec((None, bq, hw), lambda b, h, i: (b, i, seg(Z_GATE_A) + h)),
        ],
        out_specs=pl.BlockSpec((None, bq, hw), lambda b, h, i: (b, i, h)),
        out_shape=jax.ShapeDtypeStruct((bsz, length, WIDTH), BF16),
        scratch_shapes=[pltpu.VMEM((nh, length, HEAD_DIM), BF16),
                        pltpu.VMEM((nh, length, 2 * HEAD_DIM), BF16)],
        compiler_params=_params("arbitrary", "arbitrary", "arbitrary"),
        name="fox_prompt",
    )(z3, k_all, v_all, c5, z3)


def _fox_decode_kernel(q_ref, kn_ref, vn_ref, cn_ref, ga_ref, kp_ref, vp_ref, cp_ref, o_ref,
                       m_ref, acc_ref, l_ref, *, n, scale):
    j = pl.program_id(1)
    nrow = q_ref.shape[0]
    q = (q_ref[...] * (scale * LOG2E)).astype(BF16)

    @pl.when(j == 0)
    def _():
        m_ref[...] = jnp.full(m_ref.shape, -jnp.inf, F32)
        acc_ref[...] = jnp.zeros(acc_ref.shape, F32)
        l_ref[...] = jnp.zeros(l_ref.shape, F32)

    def update(kblk, vblk, c_row, visible):
        t = jnp.where(visible, _mm_nt(q, kblk) - c_row * LOG2E, -jnp.inf)
        m = m_ref[...]
        m_new = jnp.maximum(m, jnp.broadcast_to(jnp.max(t, axis=1, keepdims=True), m.shape))
        alpha = jnp.exp2(m - m_new)
        p = jnp.concatenate([jnp.exp2(t[:, c * LANES:(c + 1) * LANES] - m_new)
                             for c in range(t.shape[1] // LANES)], axis=1)
        m_ref[...] = m_new
        l_ref[...] = alpha * l_ref[...] + jnp.broadcast_to(jnp.sum(p, axis=1, keepdims=True), m.shape)
        acc_ref[...] = alpha * acc_ref[...] + _mm(p, vblk)

    cols = kp_ref.shape[0]
    row_head = lax.broadcasted_iota(jnp.int32, (nrow, cols), 0) // n
    col_head = lax.broadcasted_iota(jnp.int32, (nrow, cols), 1) % HEADS
    update(kp_ref[...], vp_ref[...], cp_ref[...], row_head == col_head)

    @pl.when(j == pl.num_programs(1) - 1)
    def _():
        r = lax.broadcasted_iota(jnp.int32, (nrow, nrow), 0)
        c = lax.broadcasted_iota(jnp.int32, (nrow, nrow), 1)
        update(kn_ref[...], vn_ref[...], cn_ref[...], (r // n == c // n) & (c % n <= r % n))
        o_ref[...] = (acc_ref[...] / l_ref[...] * _silu(ga_ref[...])).astype(BF16)


def _fox_decode(z3, k4, v4, n_tok, cache_k, cache_v, layer, cp_flat, cn_rows):
    bsz = z3.shape[0]
    past = cache_k.shape[2]
    nrow = HEADS * n_tok
    pc = min(1024, past)
    assert past % pc == 0 and nrow % LANES == 0

    def head_rows(a):
        return a.reshape(bsz, n_tok, HEADS, HEAD_DIM).transpose(0, 2, 1, 3).reshape(bsz, nrow, HEAD_DIM)

    seg = lambda s: z3[:, :n_tok, s * WIDTH:(s + 1) * WIDTH]
    k2 = cache_k.reshape(cache_k.shape[:2] + (past * HEADS, HEAD_DIM))
    v2 = cache_v.reshape(cache_v.shape[:2] + (past * HEADS, HEAD_DIM))
    rows_blk = pl.BlockSpec((None, nrow, HEAD_DIM), lambda b, j: (b, 0, 0))
    cache_blk = pl.BlockSpec((None, None, pc * HEADS, HEAD_DIM), lambda b, j: (layer, b, j, 0))
    o_rows = pl.pallas_call(
        functools.partial(_fox_decode_kernel, n=n_tok, scale=HEAD_DIM ** -0.5),
        grid=(bsz, past // pc),
        in_specs=[
            rows_blk, rows_blk, rows_blk,
            pl.BlockSpec((None, 1, nrow), lambda b, j: (b, 0, 0)),
            rows_blk, cache_blk, cache_blk,
            pl.BlockSpec((None, None, 1, pc * HEADS), lambda b, j: (layer, b, 0, j)),
        ],
        out_specs=rows_blk,
        out_shape=jax.ShapeDtypeStruct((bsz, nrow, HEAD_DIM), BF16),
        scratch_shapes=[pltpu.VMEM((nrow, LANES), F32), pltpu.VMEM((nrow, HEAD_DIM), F32),
                        pltpu.VMEM((nrow, LANES), F32)],
        compiler_params=_params("arbitrary", "arbitrary"),
        name="fox_decode",
    )(head_rows(seg(Z_Q)), head_rows(k4[layer, :, :n_tok]), head_rows(v4[layer, :, :n_tok]),
      cn_rows.reshape(bsz, 1, nrow), head_rows(seg(Z_GATE_A)), k2, v2, cp_flat)
    return o_rows.reshape(bsz, HEADS, n_tok, HEAD_DIM).transpose(0, 2, 1, 3).reshape(bsz, n_tok, WIDTH)


def _gdn_kernel(qp_ref, kp_ref, vp_ref, cw_ref, c0_ref, cols_ref, gr_ref, gb_ref, nw_ref, s0_ref,
                og_ref, s_ref, xp_ref, *, chunk, scale):
    pad = SUBLANES
    nb = qp_ref.shape[0]

    @pl.when(pl.program_id(1) == 0)
    def _():
        s_ref[...] = s0_ref[...]
        xp_ref[:, 0:pad, :] = c0_ref[...]

    convs = []
    for bb in range(nb):
        xp_ref[bb, pad:pad + chunk, 0:WIDTH] = qp_ref[bb]
        xp_ref[bb, pad:pad + chunk, WIDTH:2 * WIDTH] = kp_ref[bb]
        xp_ref[bb, pad:pad + chunk, 2 * WIDTH:3 * WIDTH] = vp_ref[bb]
        xp = xp_ref[bb]
        acc = xp * cw_ref[0:1, :]
        for i in range(1, CONV_W):
            acc = pltpu.roll(acc, 1, axis=0) + xp * cw_ref[i:i + 1, :]
        convs.append(_silu(acc[pad:pad + chunk]))
        xp_ref[bb, 0:pad, :] = xp_ref[bb, chunk:chunk + pad, :]

    row = lax.broadcasted_iota(jnp.int32, (chunk, chunk), 0)
    col = lax.broadcasted_iota(jnp.int32, (chunk, chunk), 1)
    incl = row >= col
    strict = row > col
    sizes = [INV_BASE << i for i in range(int(math.log2(chunk // INV_BASE)) + 1)]
    same = [(row >> int(math.log2(b))) == (col >> int(math.log2(b))) for b in sizes[:-1]]
    diag_mask = strict & same[0]
    off_masks = [strict & ~same[i] & (same[i + 1] if i + 1 < len(same) else True)
                 for i in range(len(same))]

    units = [(bb, hh) for bb in range(nb) for hh in range(HEADS)]
    heads = range(len(units))
    sl = lambda seg, hh: slice(seg * WIDTH + hh * HEAD_DIM, seg * WIDTH + (hh + 1) * HEAD_DIM)
    q, k, v, gc, beta, e, lmat, qk = ([None] * len(units) for _ in range(8))
    for h, (bb, hh) in enumerate(units):
        qh, kh = convs[bb][:, sl(0, hh)], convs[bb][:, sl(1, hh)]
        q[h] = qh * (lax.rsqrt(jnp.sum(qh * qh, axis=1, keepdims=True) + RMS_EPS) * scale)
        k[h] = kh * lax.rsqrt(jnp.sum(kh * kh, axis=1, keepdims=True) + RMS_EPS)
        v[h] = convs[bb][:, sl(2, hh)]
        gc[h] = cols_ref[bb, :, hh:hh + 1]
        beta[h] = cols_ref[bb, :, HEADS + hh:HEADS + hh + 1]
        e[h] = jnp.exp(gc[h])
    for h, (bb, hh) in enumerate(units):
        decay = jnp.exp(jnp.where(incl, gc[h] - gr_ref[bb, hh:hh + 1, :], -jnp.inf))
        kq = _mm_nt(jnp.concatenate([k[h] * beta[h], q[h]], axis=0), k[h])
        lmat[h] = jnp.where(strict, kq[:chunk] * decay, 0.0)
        qk[h] = kq[chunk:] * decay
    xpow = [jnp.where(diag_mask, lmat[h], 0.0) for h in heads]
    nm = [-xpow[h] for h in heads]
    for _ in range(int(math.log2(INV_BASE)) - 1):
        xpow = [_mm(xpow[h], xpow[h]) for h in heads]
        nm = [nm[h] + xpow[h] + _mm(nm[h], xpow[h]) for h in heads]
    for off in off_masks:
        loff = [jnp.where(off, lmat[h], 0.0) for h in heads]
        t_loff = [loff[h] + _mm(nm[h], loff[h]) for h in heads]
        nm = [nm[h] - (t_loff[h] + _mm(t_loff[h], nm[h])) for h in heads]
    uw = []
    for h in heads:
        rhs = jnp.concatenate([v[h] * beta[h], k[h] * (beta[h] * e[h])], axis=1)
        uw.append(rhs + _mm(nm[h], rhs))
    s_old = [s_ref[bb, hh] for bb, hh in units]
    ws_qs = [_mm(jnp.concatenate([uw[h][:, HEAD_DIM:], q[h] * e[h]], axis=0), s_old[h]) for h in heads]
    v_new = [uw[h][:, :HEAD_DIM] - ws_qs[h][:chunk] for h in heads]
    for h, (bb, hh) in enumerate(units):
        g_last = gc[h][chunk - 1:chunk, :]
        k_tail = k[h] * jnp.exp(g_last - gc[h])
        s_ref[bb, hh] = s_old[h] * jnp.exp(g_last) + _mm_tn(k_tail, v_new[h])
    for h, (bb, hh) in enumerate(units):
        o = ws_qs[h][chunk:] + _mm(qk[h], v_new[h])
        o = o * lax.rsqrt(jnp.mean(o * o, axis=1, keepdims=True) + RMS_EPS) * nw_ref[...]
        og_ref[bb, :, sl(0, hh)] = (o * _silu(gb_ref[bb, :, sl(0, hh)])).astype(BF16)


def _gdn(z3, seg_qkv, seg_gate, conv_w, conv0, cols, gc_rows, norm_w, s0):
    bsz, length, _ = z3.shape
    chunk = GDN_CHUNK
    assert length % chunk == 0
    nc = length // chunk
    nb = GDN_BATCHES if bsz % GDN_BATCHES == 0 else 1
    c0 = jnp.pad(conv0, ((0, 0), (SUBLANES - (CONV_W - 1), 0), (0, 0)))
    zblk = lambda seg: pl.BlockSpec((nb, chunk, WIDTH), lambda b, l: (b, l, seg))
    return pl.pallas_call(
        functools.partial(_gdn_kernel, chunk=chunk, scale=HEAD_DIM ** -0.5),
        grid=(bsz // nb, nc),
        in_specs=[
            zblk(seg_qkv), zblk(seg_qkv + 1), zblk(seg_qkv + 2),
            pl.BlockSpec((CONV_W, CONV_CH), lambda b, l: (0, 0)),
            pl.BlockSpec((nb, SUBLANES, CONV_CH), lambda b, l: (b, 0, 0)),
            pl.BlockSpec((nb, chunk, LANES), lambda b, l: (b, l, 0)),
            pl.BlockSpec((nb, HEADS, chunk), lambda b, l: (b, 0, l)),
            zblk(seg_gate),
            pl.BlockSpec((1, HEAD_DIM), lambda b, l: (0, 0)),
            pl.BlockSpec((nb, HEADS, HEAD_DIM, HEAD_DIM), lambda b, l: (b, 0, 0, 0)),
        ],
        out_specs=[
            pl.BlockSpec((nb, chunk, WIDTH), lambda b, l: (b, l, 0)),
            pl.BlockSpec((nb, HEADS, HEAD_DIM, HEAD_DIM), lambda b, l: (b, 0, 0, 0)),
        ],
        out_shape=[
            jax.ShapeDtypeStruct((bsz, length, WIDTH), BF16),
            jax.ShapeDtypeStruct((bsz, HEADS, HEAD_DIM, HEAD_DIM), F32),
        ],
        scratch_shapes=[pltpu.VMEM((nb, chunk + SUBLANES, CONV_CH), F32)],
        compiler_params=_params("arbitrary", "arbitrary"),
        name="gdn",
    )(z3, z3, z3, conv_w, c0, cols, gc_rows, z3, norm_w.reshape(1, HEAD_DIM), s0)


def _out_kernel(oa_ref, ob_ref, ma_ref, mb_ref, x_ref, p_ref, wfox_ref, wgdn_ref, wout_ref,
                wplp_ref, wplg_ref, vec_ref, o_ref, *, alpha):
    ya = jnp.dot(oa_ref[...], wfox_ref[...], preferred_element_type=F32)
    yb = jnp.dot(ob_ref[...], wgdn_ref[...], preferred_element_type=F32)
    merged = _sigmoid(ma_ref[...]) * ya + _sigmoid(mb_ref[...]) * yb
    y = alpha * x_ref[...] + _mm(merged, wout_ref[...])
    yc = y - jnp.mean(y, axis=1, keepdims=True)
    var = jnp.mean(yc * yc, axis=1, keepdims=True)
    x1 = yc * lax.rsqrt(var + LN_EPS) * vec_ref[0:1, :] + vec_ref[1:2, :]
    e = _mm(p_ref[...], wplp_ref[...])
    e = e * lax.rsqrt(jnp.mean(e * e, axis=1, keepdims=True) + RMS_EPS) * vec_ref[2:3, :]
    o_ref[...] = x1 + _sigmoid(_mm(x1, wplg_ref[...])) * e


def _out_block(oa, ob, z2d, x2d, p_all, layer, wfox, wgdn, wout, wplp, wplg, vec, alpha):
    t_rows, d = x2d.shape
    ple = p_all.shape[-1]
    tm = min(256, t_rows)
    dpw = d // WIDTH
    assert d % WIDTH == 0 and Z_M_A % dpw == 0 and Z_M_B % dpw == 0
    const = lambda shape: pl.BlockSpec((None,) + shape, lambda r: (layer,) + (0,) * len(shape),
                                       pipeline_mode=pl.Buffered(1))
    return pl.pallas_call(
        functools.partial(_out_kernel, alpha=alpha),
        grid=(t_rows // tm,),
        in_specs=[
            pl.BlockSpec((tm, WIDTH), lambda r: (r, 0)),
            pl.BlockSpec((tm, WIDTH), lambda r: (r, 0)),
            pl.BlockSpec((tm, d), lambda r: (r, Z_M_A // dpw)),
            pl.BlockSpec((tm, d), lambda r: (r, Z_M_B // dpw)),
            pl.BlockSpec((tm, d), lambda r: (r, 0)),
            pl.BlockSpec((None, tm, ple), lambda r: (layer, r, 0)),
            const((WIDTH, d)), const((WIDTH, d)), const((d, d)), const((ple, d)), const((d, d)),
            const((SUBLANES, d)),
        ],
        out_specs=pl.BlockSpec((tm, d), lambda r: (r, 0)),
        out_shape=jax.ShapeDtypeStruct((t_rows, d), F32),
        compiler_params=_params("arbitrary"),
        name="out_block",
    )(oa, ob, z2d, z2d, x2d, p_all, wfox, wgdn, wout, wplp, wplg, vec)


def _layer(x2d, bsz, n_tok, p_all, layer, wts, conv0, s0, alpha, k_all, v_all, fox_cache=None):
    lp = n_tok
    z2d, small, k_all, v_all = _proj(x2d, wts["w_big"], wts["w_small"], wts["par"], k_all, v_all, layer)
    z3 = z2d.reshape(bsz, lp, -1)
    k4 = k_all.reshape(-1, bsz, lp, WIDTH)
    v4 = v_all.reshape(-1, bsz, lp, WIDTH)
    small3 = small.reshape(bsz, lp, LANES)
    lg = -(-lp // GDN_CHUNK) * GDN_CHUNK
    if lg == lp:
        zg, seg_qkv, seg_gate = z3, Z_QKV_B, Z_GATE_B
    else:
        pad_rows = lambda a: jnp.pad(a, ((0, 0), (0, lg - lp), (0, 0)))
        zg = pad_rows(z3[:, :, Z_QKV_B * WIDTH:(Z_GATE_B + 1) * WIDTH])
        seg_qkv, seg_gate = 0, Z_GATE_B - Z_QKV_B
        small3 = pad_rows(small3)
    c_init = (jnp.zeros((bsz, HEADS, 1), F32) if fox_cache is None
              else fox_cache[3][layer].reshape(bsz, HEADS, 1))
    logf_rows, c_rows, gc_rows, cols = _gates(small3, n_tok, c_init)
    logf = logf_rows[:, :, :n_tok].transpose(0, 2, 1)

    if fox_cache is None:
        oa = _fox_prompt(z3, k4, v4, layer, c_rows)
    else:
        cache_k, cache_v, cp_flat, _ = fox_cache
        oa = _fox_decode(z3, k4, v4, n_tok, cache_k, cache_v, layer, cp_flat, c_rows[:, :, :n_tok])
    ob, s_new = _gdn(zg, seg_qkv, seg_gate, wts["conv_w"], conv0, cols, gc_rows, wts["norm_w"], s0)
    ob = ob[:, :lp]

    x_out = _out_block(oa.reshape(bsz * lp, WIDTH), ob.reshape(bsz * lp, WIDTH), z2d, x2d, p_all, layer,
                       wts["wfox"], wts["wgdn"], wts["wout"], wts["wplp"], wts["wplg"], wts["vec"], alpha)

    new_conv = z3[:, n_tok - (CONV_W - 1):n_tok, Z_QKV_B * WIDTH:Z_QKV_B * WIDTH + CONV_CH]
    return x_out, k_all, v_all, logf, new_conv, s_new


def kernel(x_prompt, x_sample, cache_fox_k, cache_fox_v, cache_fox_logf, state_gdn_conv, state_gdn, p_prompt, p_sample, w_in, fox_f_bias, gdn_conv_w, gdn_a_log, gdn_dt_bias, gdn_norm_w, w_out_fox, w_out_gdn, w_out, ln_g, ln_b, w_pl_proj, w_pl_gate, pl_norm_w):
    depth = w_in.shape[0]
    bp, seq, d = x_prompt.shape
    bs, dseq, _ = x_sample.shape
    past = cache_fox_k.shape[2]
    ple = p_prompt.shape[-1]
    assert d % WIDTH == 0 and dseq >= CONV_W - 1 and seq >= CONV_W - 1
    assert dseq % SUBLANES == 0 and seq % SUBLANES == 0
    alpha = (2 * depth) ** 0.25

    o_f = 3 * WIDTH
    o_ga = o_f + HEADS
    o_qkv = o_ga + WIDTH
    o_a = o_qkv + CONV_CH
    o_gb = o_a + 2 * HEADS
    w_in_t = jnp.swapaxes(w_in, 1, 2).astype(BF16)
    w_big = jnp.concatenate([w_in_t[:, :WIDTH], w_in_t[:, o_ga:o_a], w_in_t[:, o_gb:],
                             w_in_t[:, WIDTH:o_f]], axis=1)
    w_small = jnp.concatenate([w_in[..., o_f:o_ga], w_in[..., o_a:o_gb],
                               jnp.zeros((depth, d, LANES - 3 * HEADS), F32)], axis=-1)
    w_small_hi = w_small.astype(BF16)
    w_small = jnp.stack([w_small_hi, (w_small - w_small_hi.astype(F32)).astype(BF16)], axis=1)
    zpad = jnp.zeros((depth, LANES - 2 * HEADS), F32)
    par = jnp.stack([jnp.concatenate([fox_f_bias, gdn_dt_bias, zpad], axis=-1),
                     jnp.concatenate([jnp.zeros_like(fox_f_bias), gdn_a_log, zpad], axis=-1)], axis=1)
    par = jnp.pad(par, ((0, 0), (0, SUBLANES - 2), (0, 0)))
    vec = jnp.pad(jnp.stack([ln_g, ln_b, pl_norm_w], axis=1), ((0, 0), (0, SUBLANES - 3), (0, 0)))
    wfox, wgdn, wout = w_out_fox.astype(BF16), w_out_gdn.astype(BF16), w_out.astype(BF16)
    wplp, wplg = w_pl_proj.astype(BF16), w_pl_gate.astype(BF16)

    def layer_weights(i):
        return dict(w_big=w_big, w_small=w_small, par=par, conv_w=gdn_conv_w[i],
                    norm_w=gdn_norm_w[i], wfox=wfox, wgdn=wgdn, wout=wout,
                    wplp=wplp, wplg=wplg, vec=vec)

    xs = x_sample.reshape(bs * dseq, d)
    ps_all = p_sample.reshape(depth, bs * dseq, ple)
    xp = x_prompt.reshape(bp * seq, d)
    pp_all = p_prompt.reshape(depth, bp * seq, ple)

    cp_rows = _cumsum_rows(cache_fox_logf.transpose(0, 1, 3, 2).reshape(depth * bs * HEADS, past))
    cp_rows = cp_rows.reshape(depth, bs, HEADS, past)
    cp_flat = cp_rows.transpose(0, 1, 3, 2).reshape(depth, bs, 1, past * HEADS)
    cp_last = cp_rows[..., past - 1:].reshape(depth, bs * HEADS, 1)

    conv0_p = jnp.zeros((bp, CONV_W - 1, CONV_CH), F32)
    s0_p = jnp.zeros((bp, HEADS, HEAD_DIM, HEAD_DIM), F32)
    kp = lax.empty((depth, bp * seq, WIDTH), F32)
    vp = lax.empty((depth, bp * seq, WIDTH), F32)
    ks = lax.empty((depth, bs * dseq, WIDTH), F32)
    vs = lax.empty((depth, bs * dseq, WIDTH), F32)
    outs_p, outs_s = [], []
    for i in range(depth):
        wts = layer_weights(i)
        xp, kp, vp, *rest_p = _layer(xp, bp, seq, pp_all, i, wts, conv0_p, s0_p, alpha, kp, vp)
        xs, ks, vs, *rest_s = _layer(xs, bs, dseq, ps_all, i, wts, state_gdn_conv[i], state_gdn[i], alpha,
                                     ks, vs, fox_cache=(cache_fox_k, cache_fox_v, cp_flat, cp_last))
        outs_p.append(rest_p)
        outs_s.append(rest_s)
    stack = lambda outs, j: jnp.stack([o[j] for o in outs])
    heads_p = lambda a: a.reshape(depth, bp, seq, HEADS, HEAD_DIM)
    heads_s = lambda a: a.reshape(depth, bs, dseq, HEADS, HEAD_DIM)
    y_prompt = xp.reshape(bp, seq, d)
    y_sample = xs.reshape(bs, dseq, d)
    return (y_prompt, y_sample,
            heads_p(kp), heads_p(vp), stack(outs_p, 0), stack(outs_p, 1), stack(outs_p, 2),
            heads_s(ks), heads_s(vs), stack(outs_s, 0), stack(outs_s, 1), stack(outs_s, 2))
```

```python
import functools
import math

import jax
import jax.numpy as jnp
from jax import lax
from jax.experimental import pallas as pl
from jax.experimental.pallas import tpu as pltpu

F32 = jnp.float32
BF16 = jnp.bfloat16

HEADS = 8
HEAD_DIM = 128
WIDTH = HEADS * HEAD_DIM
CONV_W = 4
CONV_CH = 3 * WIDTH
LN_EPS = 1e-5
RMS_EPS = 1e-6
LOG2E = math.log2(math.e)
LANES = 128
SUBLANES = 8
GDN_CHUNK = 128
GDN_BATCHES = 2
INV_BASE = 8
FOX_BQ = 4096
FOX_BK = 1024
FOX_ROWS = 256
FOX_HEADS_PER_STEP = 1
FOX_STRIP = 16
VMEM_LIMIT = 56 * 1024 * 1024

Z_Q, Z_GATE_A, Z_QKV_B, Z_GATE_B, Z_M_A, Z_M_B, Z_END = 0, 1, 2, 5, 6, 8, 10


def _params(*sem):
    return pltpu.CompilerParams(dimension_semantics=sem, vmem_limit_bytes=VMEM_LIMIT)


def _mm(a, b):
    return jnp.dot(a.astype(BF16), b.astype(BF16), preferred_element_type=F32)


def _mm_nt(a, b):
    return lax.dot_general(a.astype(BF16), b.astype(BF16), (((1,), (1,)), ((), ())),
                           preferred_element_type=F32)


def _mm_tn(a, b):
    return lax.dot_general(a.astype(BF16), b.astype(BF16), (((0,), (0,)), ((), ())),
                           preferred_element_type=F32)


def _sigmoid(x):
    return 0.5 * jnp.tanh(0.5 * x) + 0.5


def _silu(x):
    return x * _sigmoid(x)


def _proj_kernel(x_ref, w_ref, ws_ref, par_ref, k_in_ref, v_in_ref, z_ref, s_ref, k_ref, v_ref, xb_ref,
                 *, z_tiles, kv_tiles):
    del k_in_ref, v_in_ref
    j = pl.program_id(1)

    @pl.when(j == 0)
    def _():
        x = x_ref[...]
        xb = x.astype(BF16)
        xb_ref[...] = xb
        x_lo = (x - xb.astype(F32)).astype(BF16)
        raw = (jnp.dot(xb, ws_ref[0], preferred_element_type=F32)
               + jnp.dot(x_lo, ws_ref[0], preferred_element_type=F32)
               + jnp.dot(xb, ws_ref[1], preferred_element_type=F32))
        y = raw + par_ref[0:1, :]
        neg_a = -jnp.exp(par_ref[1:2, :])
        t = jnp.log1p(jnp.exp(-jnp.abs(y)))
        logf = jnp.minimum(y, 0.0) - t
        g = neg_a * (jnp.maximum(y, 0.0) + t)
        beta = _sigmoid(raw)
        lane = lax.broadcasted_iota(jnp.int32, raw.shape, 1)
        s_ref[...] = jnp.where(lane < HEADS, logf,
                               jnp.where(lane < 2 * HEADS, g,
                                         jnp.where(lane < 3 * HEADS, beta, 0.0)))

    @pl.when(j < z_tiles)
    def _():
        z_ref[...] = _mm_nt(xb_ref[...], w_ref[...])

    @pl.when((j >= z_tiles) & (j < z_tiles + kv_tiles))
    def _():
        k_ref[...] = _mm_nt(xb_ref[...], w_ref[...])

    @pl.when(j >= z_tiles + kv_tiles)
    def _():
        v_ref[...] = _mm_nt(xb_ref[...], w_ref[...])


def _proj(x2d, w_big, w_small, par, k_all, v_all, layer):
    t_rows, d = x2d.shape
    tm = min(1024, t_rows)
    tn = 512
    kv_tiles = WIDTH // tn
    z_tiles = w_big.shape[1] // tn - 2 * kv_tiles
    clamp = lambda j, lo: jnp.clip(j - lo, 0, kv_tiles - 1)
    return pl.pallas_call(
        functools.partial(_proj_kernel, z_tiles=z_tiles, kv_tiles=kv_tiles),
        grid=(t_rows // tm, z_tiles + 2 * kv_tiles),
        in_specs=[
            pl.BlockSpec((tm, d), lambda i, j: (i, 0)),
            pl.BlockSpec((None, tn, d), lambda i, j: (layer, j, 0)),
            pl.BlockSpec((None, 2, d, LANES), lambda i, j: (layer, 0, 0, 0)),
            pl.BlockSpec((None, SUBLANES, LANES), lambda i, j: (layer, 0, 0)),
            pl.BlockSpec(memory_space=pl.ANY),
            pl.BlockSpec(memory_space=pl.ANY),
        ],
        out_specs=[
            pl.BlockSpec((tm, tn), lambda i, j: (i, jnp.minimum(j, z_tiles - 1))),
            pl.BlockSpec((tm, LANES), lambda i, j: (i, 0)),
            pl.BlockSpec((None, tm, tn), lambda i, j: (layer, i, clamp(j, z_tiles))),
            pl.BlockSpec((None, tm, tn), lambda i, j: (layer, i, clamp(j, z_tiles + kv_tiles))),
        ],
        out_shape=[
            jax.ShapeDtypeStruct((t_rows, z_tiles * tn), F32),
            jax.ShapeDtypeStruct((t_rows, LANES), F32),
            jax.ShapeDtypeStruct(k_all.shape, F32),
            jax.ShapeDtypeStruct(v_all.shape, F32),
        ],
        input_output_aliases={4: 2, 5: 3},
        scratch_shapes=[pltpu.VMEM((tm, d), BF16)],
        compiler_params=_params("arbitrary", "arbitrary"),
        name="proj",
    )(x2d, w_big, w_small, par, k_all, v_all)


def _cumsum_kernel(x_ref, init_ref, o_ref, carry_ref, *, blk, period):
    r = lax.broadcasted_iota(jnp.int32, (blk, blk), 0)
    c = lax.broadcasted_iota(jnp.int32, (blk, blk), 1)
    keep = r <= c
    if period is not None:
        keep = keep & ((r // period) == (c // period))
    y = jnp.dot(x_ref[...], keep.astype(F32), precision=lax.Precision.HIGHEST,
                preferred_element_type=F32)
    if period is None:
        @pl.when(pl.program_id(0) == 0)
        def _():
            carry_ref[...] = init_ref[...]

        y = y + carry_ref[...]
        carry_ref[...] = y[:, blk - 1:blk]
    o_ref[...] = y


def _cumsum_rows(x, period=None, init=None):
    rows, length = x.shape
    blk = min(512, length)
    assert length % blk == 0 and (period is None or (blk % period == 0 and init is None))
    if init is None:
        init = jnp.zeros((rows, 1), F32)
    return pl.pallas_call(
        functools.partial(_cumsum_kernel, blk=blk, period=period),
        grid=(length // blk,),
        in_specs=[pl.BlockSpec((rows, blk), lambda j: (0, j)), pl.BlockSpec((rows, 1), lambda j: (0, 0))],
        out_specs=pl.BlockSpec((rows, blk), lambda j: (0, j)),
        out_shape=jax.ShapeDtypeStruct((rows, length), F32),
        scratch_shapes=[pltpu.VMEM((rows, 1), F32)],
        compiler_params=_params("arbitrary"),
        name="cumsum",
    )(x, init)


def _gates_kernel(s_ref, init_ref, lf_ref, c_ref, gc_ref, cols_ref, carry_ref, *, blk, period, n_tok):
    j = pl.program_id(1)

    @pl.when(j == 0)
    def _():
        carry_ref[...] = init_ref[...]

    x = s_ref[...]
    pos = j * blk + lax.broadcasted_iota(jnp.int32, x.shape, 0)
    lane = lax.broadcasted_iota(jnp.int32, x.shape, 1)
    x = jnp.where((pos >= n_tok) & (lane >= HEADS), 0.0, x)
    xt = x.T
    r = lax.broadcasted_iota(jnp.int32, (blk, blk), 0)
    c = lax.broadcasted_iota(jnp.int32, (blk, blk), 1)
    upper = r <= c
    local = upper & ((r // period) == (c // period))
    rows = xt[0:2 * HEADS]
    run = jnp.dot(rows, upper.astype(F32), precision=lax.Precision.HIGHEST, preferred_element_type=F32)
    loc = jnp.dot(rows, local.astype(F32), precision=lax.Precision.HIGHEST, preferred_element_type=F32)
    csum = run[0:HEADS] + carry_ref[...]
    carry_ref[...] = csum[:, blk - 1:blk]
    gc = loc[HEADS:2 * HEADS]
    lf_ref[...] = xt[0:HEADS]
    c_ref[...] = csum
    gc_ref[...] = gc
    cols_t = jnp.concatenate([gc, xt[2 * HEADS:3 * HEADS], jnp.zeros((LANES - 2 * HEADS, blk), F32)], axis=0)
    cols_ref[...] = cols_t.T


def _gates(small3, n_tok, init):
    bsz, lp, _ = small3.shape
    blk = min(512, lp)
    assert lp % blk == 0 and blk % GDN_CHUNK == 0
    rows_blk = pl.BlockSpec((None, HEADS, blk), lambda b, j: (b, 0, j))
    return pl.pallas_call(
        functools.partial(_gates_kernel, blk=blk, period=GDN_CHUNK, n_tok=n_tok),
        grid=(bsz, lp // blk),
        in_specs=[pl.BlockSpec((None, blk, LANES), lambda b, j: (b, j, 0)),
                  pl.BlockSpec((None, HEADS, 1), lambda b, j: (b, 0, 0))],
        out_specs=[rows_blk, rows_blk, rows_blk, pl.BlockSpec((None, blk, LANES), lambda b, j: (b, j, 0))],
        out_shape=[jax.ShapeDtypeStruct((bsz, HEADS, lp), F32)] * 3
                  + [jax.ShapeDtypeStruct((bsz, lp, LANES), F32)],
        scratch_shapes=[pltpu.VMEM((HEADS, 1), F32)],
        compiler_params=_params("arbitrary", "arbitrary"),
        name="gates",
    )(small3, init)


def _fox_kernel(q_ref, k_ref, v_ref, c_ref, ga_ref, o_ref, kb_ref, vb_ref, *, bq, bk, rows, scale):
    qi = pl.program_id(2)
    length = k_ref.shape[0]
    nh = kb_ref.shape[0]
    head = lambda hh: slice(hh * HEAD_DIM, (hh + 1) * HEAD_DIM)
    units = [(hh, r * rows) for hh in range(nh) for r in range(bq // rows)]

    @pl.when(qi == 0)
    def _():
        for hh in range(nh):
            kb_ref[hh] = k_ref[:, head(hh)].astype(BF16)
            vb_ref[hh, :, :HEAD_DIM] = v_ref[:, head(hh)].astype(BF16)
            vb_ref[hh, :, HEAD_DIM:] = jnp.ones((length, LANES), BF16)

    q = (q_ref[...] * (scale * LOG2E)).astype(BF16)
    qs = [q[r0:r0 + rows, head(hh)] for hh, r0 in units]

    def update(kj, carries, diag_offset=None):
        start = pl.multiple_of(kj * bk, bk)
        widths = {u: bk if diag_offset is None else min(bk, r0 + rows - diag_offset)
                  for u, (_, r0) in enumerate(units)}
        widths = {u: w for u, w in widths.items() if w > 0}
        keyed = sorted({(units[u][0], w) for u, w in widths.items()})
        kblks = {(hh, w): kb_ref[hh, pl.ds(start, w), :] for hh, w in keyed}
        cks = {(hh, w): c_ref[hh, kj, :, 0:w] * LOG2E for hh, w in keyed}
        ss = {u: lax.dot_general(qs[u], kblks[units[u][0], w], (((1,), (1,)), ((), ())),
                                 preferred_element_type=F32)
              for u, w in widths.items()}
        out = list(carries)
        for u, s in ss.items():
            m, acc, l = carries[u]
            hh, r0 = units[u]
            width = widths[u]
            lane_blocks = [slice(c * LANES, (c + 1) * LANES) for c in range(width // LANES)]
            masked = diag_offset is not None and diag_offset + width - 1 > r0
            m_parts, a_parts, p_parts = [], [], []
            for i in range(rows // FOX_STRIP):
                sl = slice(i * FOX_STRIP, (i + 1) * FOX_STRIP)
                t = s[sl] - cks[hh, width]
                if masked:
                    row = lax.broadcasted_iota(jnp.int32, (FOX_STRIP, width), 0)
                    col = lax.broadcasted_iota(jnp.int32, (FOX_STRIP, width), 1)
                    t = jnp.where(col + (diag_offset - r0 - i * FOX_STRIP) <= row, t, -jnp.inf)
                mx = t[:, lane_blocks[0]]
                for c in lane_blocks[1:]:
                    mx = jnp.maximum(mx, t[:, c])
                m_new = jnp.maximum(m[sl], jnp.broadcast_to(jnp.max(mx, axis=1, keepdims=True), mx.shape))
                m_parts.append(m_new)
                a_parts.append(jnp.exp2(m[sl] - m_new))
                p_parts.append(jnp.concatenate([jnp.exp2(t[:, c] - m_new).astype(BF16) for c in lane_blocks],
                                               axis=1))
            alpha = jnp.concatenate(a_parts, axis=0)
            pv = jnp.dot(jnp.concatenate(p_parts, axis=0), vb_ref[hh, pl.ds(start, width), :],
                         preferred_element_type=F32)
            out[u] = (jnp.concatenate(m_parts, axis=0), alpha * acc + pv[:, :HEAD_DIM],
                      alpha * l + pv[:, HEAD_DIM:])
        return tuple(out)

    init = (jnp.full((rows, LANES), -jnp.inf, F32), jnp.zeros((rows, HEAD_DIM), F32),
            jnp.zeros((rows, LANES), F32))
    per_q = bq // bk
    def past_q_block(j, cr):
        for d in range(per_q):
            cr = update(j * per_q + d, cr)
        return cr

    carries = lax.fori_loop(0, qi, past_q_block, (init,) * len(units))
    for d in range(per_q):
        carries = update(qi * per_q + d, carries, diag_offset=d * bk)
    for (hh, r0), (_, acc, l) in zip(units, carries):
        rs = slice(r0, r0 + rows)
        o_ref[rs, head(hh)] = (acc / l * _silu(ga_ref[rs, head(hh)])).astype(BF16)


def _fox_prompt(z3, k_all, v_all, layer, c_rows):
    bsz, length, _ = z3.shape
    bq = min(FOX_BQ, length)
    bk = min(FOX_BK, bq)
    rows = min(FOX_ROWS, bq)
    assert length % bq == 0 and bq % bk == 0 and bq % rows == 0
    nk = length // bk
    c5 = c_rows.reshape(bsz, HEADS, nk, 1, bk)
    nh = FOX_HEADS_PER_STEP
    hw = nh * HEAD_DIM
    assert HEADS % nh == 0 and WIDTH % hw == 0
    seg = lambda s: s * (WIDTH // hw)
    return pl.pallas_call(
        functools.partial(_fox_kernel, bq=bq, bk=bk, rows=rows, scale=HEAD_DIM ** -0.5),
        grid=(bsz, HEADS // nh, length // bq),
        in_specs=[
            pl.BlockSpec((None, bq, hw), lambda b, h, i: (b, i, seg(Z_Q) + h)),
            pl.BlockSpec((None, None, length, hw), lambda b, h, i: (layer, b, 0, h)),
            pl.BlockSpec((None, None, length, hw), lambda b, h, i: (layer, b, 0, h)),
            pl.BlockSpec((None, nh, nk, 1, bk), lambda b, h, i: (b, h, 0, 0, 0)),
            pl.BlockSpec((None, bq, hw), lambda b, h, i: (b, i, seg(Z_GATE_A) + h)),
        ],
        out_specs=pl.BlockSpec((None, bq, hw), lambda b, h, i: (b, i, h)),
        out_shape=jax.ShapeDtypeStruct((bsz, length, WIDTH), BF16),
        scratch_shapes=[pltpu.VMEM((nh, length, HEAD_DIM), BF16),
                        pltpu.VMEM((nh, length, 2 * HEAD_DIM), BF16)],
        compiler_params=_params("arbitrary", "arbitrary", "arbitrary"),
        name="fox_prompt",
    )(z3, k_all, v_all, c5, z3)


def _fox_decode_kernel(q_ref, kn_ref, vn_ref, cn_ref, ga_ref, kp_ref, vp_ref, cp_ref, o_ref,
                       m_ref, acc_ref, l_ref, *, n, scale):
    j = pl.program_id(1)
    nrow = q_ref.shape[0]
    q = (q_ref[...] * (scale * LOG2E)).astype(BF16)

    @pl.when(j == 0)
    def _():
        m_ref[...] = jnp.full(m_ref.shape, -jnp.inf, F32)
        acc_ref[...] = jnp.zeros(acc_ref.shape, F32)
        l_ref[...] = jnp.zeros(l_ref.shape, F32)

    def update(kblk, vblk, c_row, visible):
        t = jnp.where(visible, _mm_nt(q, kblk) - c_row * LOG2E, -jnp.inf)
        m = m_ref[...]
        m_new = jnp.maximum(m, jnp.broadcast_to(jnp.max(t, axis=1, keepdims=True), m.shape))
        alpha = jnp.exp2(m - m_new)
        p = jnp.concatenate([jnp.exp2(t[:, c * LANES:(c + 1) * LANES] - m_new)
                             for c in range(t.shape[1] // LANES)], axis=1)
        m_ref[...] = m_new
        l_ref[...] = alpha * l_ref[...] + jnp.broadcast_to(jnp.sum(p, axis=1, keepdims=True), m.shape)
        acc_ref[...] = alpha * acc_ref[...] + _mm(p, vblk)

    cols = kp_ref.shape[0]
    row_head = lax.broadcasted_iota(jnp.int32, (nrow, cols), 0) // n
    col_head = lax.broadcasted_iota(jnp.int32, (nrow, cols), 1) % HEADS
    update(kp_ref[...], vp_ref[...], cp_ref[...], row_head == col_head)

    @pl.when(j == pl.num_programs(1) - 1)
    def _():
        r = lax.broadcasted_iota(jnp.int32, (nrow, nrow), 0)
        c = lax.broadcasted_iota(jnp.int32, (nrow, nrow), 1)
        update(kn_ref[...], vn_ref[...], cn_ref[...], (r // n == c // n) & (c % n <= r % n))
        o_ref[...] = (acc_ref[...] / l_ref[...] * _silu(ga_ref[...])).astype(BF16)


def _fox_decode(z3, k4, v4, n_tok, cache_k, cache_v, layer, cp_flat, cn_rows):
    bsz = z3.shape[0]
    past = cache_k.shape[2]
    nrow = HEADS * n_tok
    pc = min(1024, past)
    assert past % pc == 0 and nrow % LANES == 0

    def head_rows(a):
        return a.reshape(bsz, n_tok, HEADS, HEAD_DIM).transpose(0, 2, 1, 3).reshape(bsz, nrow, HEAD_DIM)

    seg = lambda s: z3[:, :n_tok, s * WIDTH:(s + 1) * WIDTH]
    k2 = cache_k.reshape(cache_k.shape[:2] + (past * HEADS, HEAD_DIM))
    v2 = cache_v.reshape(cache_v.shape[:2] + (past * HEADS, HEAD_DIM))
    rows_blk = pl.BlockSpec((None, nrow, HEAD_DIM), lambda b, j: (b, 0, 0))
    cache_blk = pl.BlockSpec((None, None, pc * HEADS, HEAD_DIM), lambda b, j: (layer, b, j, 0))
    o_rows = pl.pallas_call(
        functools.partial(_fox_decode_kernel, n=n_tok, scale=HEAD_DIM ** -0.5),
        grid=(bsz, past // pc),
        in_specs=[
            rows_blk, rows_blk, rows_blk,
            pl.BlockSpec((None, 1, nrow), lambda b, j: (b, 0, 0)),
            rows_blk, cache_blk, cache_blk,
            pl.BlockSpec((None, None, 1, pc * HEADS), lambda b, j: (layer, b, 0, j)),
        ],
        out_specs=rows_blk,
        out_shape=jax.ShapeDtypeStruct((bsz, nrow, HEAD_DIM), BF16),
        scratch_shapes=[pltpu.VMEM((nrow, LANES), F32), pltpu.VMEM((nrow, HEAD_DIM), F32),
                        pltpu.VMEM((nrow, LANES), F32)],
        compiler_params=_params("arbitrary", "arbitrary"),
        name="fox_decode",
    )(head_rows(seg(Z_Q)), head_rows(k4[layer, :, :n_tok]), head_rows(v4[layer, :, :n_tok]),
      cn_rows.reshape(bsz, 1, nrow), head_rows(seg(Z_GATE_A)), k2, v2, cp_flat)
    return o_rows.reshape(bsz, HEADS, n_tok, HEAD_DIM).transpose(0, 2, 1, 3).reshape(bsz, n_tok, WIDTH)


def _gdn_kernel(qp_ref, kp_ref, vp_ref, cw_ref, c0_ref, cols_ref, gr_ref, gb_ref, nw_ref, s0_ref,
                og_ref, s_ref, xp_ref, *, chunk, scale):
    pad = SUBLANES
    nb = qp_ref.shape[0]

    @pl.when(pl.program_id(1) == 0)
    def _():
        s_ref[...] = s0_ref[...]
        xp_ref[:, 0:pad, :] = c0_ref[...]

    convs = []
    for bb in range(nb):
        xp_ref[bb, pad:pad + chunk, 0:WIDTH] = qp_ref[bb]
        xp_ref[bb, pad:pad + chunk, WIDTH:2 * WIDTH] = kp_ref[bb]
        xp_ref[bb, pad:pad + chunk, 2 * WIDTH:3 * WIDTH] = vp_ref[bb]
        xp = xp_ref[bb]
        acc = xp * cw_ref[0:1, :]
        for i in range(1, CONV_W):
            acc = pltpu.roll(acc, 1, axis=0) + xp * cw_ref[i:i + 1, :]
        convs.append(_silu(acc[pad:pad + chunk]))
        xp_ref[bb, 0:pad, :] = xp_ref[bb, chunk:chunk + pad, :]

    row = lax.broadcasted_iota(jnp.int32, (chunk, chunk), 0)
    col = lax.broadcasted_iota(jnp.int32, (chunk, chunk), 1)
    incl = row >= col
    strict = row > col
    sizes = [INV_BASE << i for i in range(int(math.log2(chunk // INV_BASE)) + 1)]
    same = [(row >> int(math.log2(b))) == (col >> int(math.log2(b))) for b in sizes[:-1]]
    diag_mask = strict & same[0]
    off_masks = [strict & ~same[i] & (same[i + 1] if i + 1 < len(same) else True)
                 for i in range(len(same))]

    units = [(bb, hh) for bb in range(nb) for hh in range(HEADS)]
    heads = range(len(units))
    sl = lambda seg, hh: slice(seg * WIDTH + hh * HEAD_DIM, seg * WIDTH + (hh + 1) * HEAD_DIM)
    q, k, v, gc, beta, e, lmat, qk = ([None] * len(units) for _ in range(8))
    for h, (bb, hh) in enumerate(units):
        qh, kh = convs[bb][:, sl(0, hh)], convs[bb][:, sl(1, hh)]
        q[h] = qh * (lax.rsqrt(jnp.sum(qh * qh, axis=1, keepdims=True) + RMS_EPS) * scale)
        k[h] = kh * lax.rsqrt(jnp.sum(kh * kh, axis=1, keepdims=True) + RMS_EPS)
        v[h] = convs[bb][:, sl(2, hh)]
        gc[h] = cols_ref[bb, :, hh:hh + 1]
        beta[h] = cols_ref[bb, :, HEADS + hh:HEADS + hh + 1]
        e[h] = jnp.exp(gc[h])
    for h, (bb, hh) in enumerate(units):
        decay = jnp.exp(jnp.where(incl, gc[h] - gr_ref[bb, hh:hh + 1, :], -jnp.inf))
        kq = _mm_nt(jnp.concatenate([k[h] * beta[h], q[h]], axis=0), k[h])
        lmat[h] = jnp.where(strict, kq[:chunk] * decay, 0.0)
        qk[h] = jnp.where(incl, kq[chunk:] * decay, 0.0)
    xpow = [jnp.where(diag_mask, lmat[h], 0.0) for h in heads]
    nm = [-xpow[h] for h in heads]
    for _ in range(int(math.log2(INV_BASE)) - 1):
        xpow = [_mm(xpow[h], xpow[h]) for h in heads]
        nm = [nm[h] + xpow[h] + _mm(nm[h], xpow[h]) for h in heads]
    for off in off_masks:
        loff = [jnp.where(off, lmat[h], 0.0) for h in heads]
        t_loff = [loff[h] + _mm(nm[h], loff[h]) for h in heads]
        nm = [nm[h] - (t_loff[h] + _mm(t_loff[h], nm[h])) for h in heads]
    uw = []
    for h in heads:
        rhs = jnp.concatenate([v[h] * beta[h], k[h] * (beta[h] * e[h])], axis=1)
        uw.append(rhs + _mm(nm[h], rhs))
    s_old = [s_ref[bb, hh] for bb, hh in units]
    ws_qs = [_mm(jnp.concatenate([uw[h][:, HEAD_DIM:], q[h] * e[h]], axis=0), s_old[h]) for h in heads]
    v_new = [uw[h][:, :HEAD_DIM] - ws_qs[h][:chunk] for h in heads]
    for h, (bb, hh) in enumerate(units):
        g_last = gc[h][chunk - 1:chunk, :]
        k_tail = k[h] * jnp.exp(g_last - gc[h])
        s_ref[bb, hh] = s_old[h] * jnp.exp(g_last) + _mm_tn(k_tail, v_new[h])
    for h, (bb, hh) in enumerate(units):
        o = ws_qs[h][chunk:] + _mm(qk[h], v_new[h])
        o = o * lax.rsqrt(jnp.mean(o * o, axis=1, keepdims=True) + RMS_EPS) * nw_ref[...]
        og_ref[bb, :, sl(0, hh)] = (o * _silu(gb_ref[bb, :, sl(0, hh)])).astype(BF16)


def _gdn(z3, conv_w, conv0, cols, gc_rows, norm_w, s0):
    bsz, length, _ = z3.shape
    chunk = GDN_CHUNK
    assert length % chunk == 0
    nc = length // chunk
    nb = GDN_BATCHES if bsz % GDN_BATCHES == 0 else 1
    c0 = jnp.pad(conv0, ((0, 0), (SUBLANES - (CONV_W - 1), 0), (0, 0)))
    zblk = lambda seg: pl.BlockSpec((nb, chunk, WIDTH), lambda b, l: (b, l, seg))
    return pl.pallas_call(
        functools.partial(_gdn_kernel, chunk=chunk, scale=HEAD_DIM ** -0.5),
        grid=(bsz // nb, nc),
        in_specs=[
            zblk(Z_QKV_B), zblk(Z_QKV_B + 1), zblk(Z_QKV_B + 2),
            pl.BlockSpec((CONV_W, CONV_CH), lambda b, l: (0, 0)),
            pl.BlockSpec((nb, SUBLANES, CONV_CH), lambda b, l: (b, 0, 0)),
            pl.BlockSpec((nb, chunk, LANES), lambda b, l: (b, l, 0)),
            pl.BlockSpec((nb, HEADS, chunk), lambda b, l: (b, 0, l)),
            zblk(Z_GATE_B),
            pl.BlockSpec((1, HEAD_DIM), lambda b, l: (0, 0)),
            pl.BlockSpec((nb, HEADS, HEAD_DIM, HEAD_DIM), lambda b, l: (b, 0, 0, 0)),
        ],
        out_specs=[
            pl.BlockSpec((nb, chunk, WIDTH), lambda b, l: (b, l, 0)),
            pl.BlockSpec((nb, HEADS, HEAD_DIM, HEAD_DIM), lambda b, l: (b, 0, 0, 0)),
        ],
        out_shape=[
            jax.ShapeDtypeStruct((bsz, length, WIDTH), BF16),
            jax.ShapeDtypeStruct((bsz, HEADS, HEAD_DIM, HEAD_DIM), F32),
        ],
        scratch_shapes=[pltpu.VMEM((nb, chunk + SUBLANES, CONV_CH), F32)],
        compiler_params=_params("arbitrary", "arbitrary"),
        name="gdn",
    )(z3, z3, z3, conv_w, c0, cols, gc_rows, z3, norm_w.reshape(1, HEAD_DIM), s0)


def _out_kernel(oa_ref, ob_ref, ma_ref, mb_ref, x_ref, p_ref, wfox_ref, wgdn_ref, wout_ref,
                wplp_ref, wplg_ref, vec_ref, o_ref, *, alpha):
    ya = jnp.dot(oa_ref[...], wfox_ref[...], preferred_element_type=F32)
    yb = jnp.dot(ob_ref[...], wgdn_ref[...], preferred_element_type=F32)
    merged = _sigmoid(ma_ref[...]) * ya + _sigmoid(mb_ref[...]) * yb
    y = alpha * x_ref[...] + _mm(merged, wout_ref[...])
    yc = y - jnp.mean(y, axis=1, keepdims=True)
    var = jnp.mean(yc * yc, axis=1, keepdims=True)
    x1 = yc * lax.rsqrt(var + LN_EPS) * vec_ref[0:1, :] + vec_ref[1:2, :]
    e = _mm(p_ref[...], wplp_ref[...])
    e = e * lax.rsqrt(jnp.mean(e * e, axis=1, keepdims=True) + RMS_EPS) * vec_ref[2:3, :]
    o_ref[...] = x1 + _sigmoid(_mm(x1, wplg_ref[...])) * e


def _out_block(oa, ob, z2d, x2d, p_all, layer, wfox, wgdn, wout, wplp, wplg, vec, alpha):
    t_rows, d = x2d.shape
    ple = p_all.shape[-1]
    tm = min(256, t_rows)
    dpw = d // WIDTH
    assert d % WIDTH == 0 and Z_M_A % dpw == 0 and Z_M_B % dpw == 0
    const = lambda shape: pl.BlockSpec((None,) + shape, lambda r: (layer,) + (0,) * len(shape),
                                       pipeline_mode=pl.Buffered(1))
    return pl.pallas_call(
        functools.partial(_out_kernel, alpha=alpha),
        grid=(t_rows // tm,),
        in_specs=[
            pl.BlockSpec((tm, WIDTH), lambda r: (r, 0)),
            pl.BlockSpec((tm, WIDTH), lambda r: (r, 0)),
            pl.BlockSpec((tm, d), lambda r: (r, Z_M_A // dpw)),
            pl.BlockSpec((tm, d), lambda r: (r, Z_M_B // dpw)),
            pl.BlockSpec((tm, d), lambda r: (r, 0)),
            pl.BlockSpec((None, tm, ple), lambda r: (layer, r, 0)),
            const((WIDTH, d)), const((WIDTH, d)), const((d, d)), const((ple, d)), const((d, d)),
            const((SUBLANES, d)),
        ],
        out_specs=pl.BlockSpec((tm, d), lambda r: (r, 0)),
        out_shape=jax.ShapeDtypeStruct((t_rows, d), F32),
        compiler_params=_params("arbitrary"),
        name="out_block",
    )(oa, ob, z2d, z2d, x2d, p_all, wfox, wgdn, wout, wplp, wplg, vec)


def _layer(x2d, bsz, n_tok, p_all, layer, wts, conv0, s0, alpha, k_all, v_all, fox_cache=None):
    lp = x2d.shape[0] // bsz
    z2d, small, k_all, v_all = _proj(x2d, wts["w_big"], wts["w_small"], wts["par"], k_all, v_all, layer)
    z3 = z2d.reshape(bsz, lp, -1)
    k4 = k_all.reshape(-1, bsz, lp, WIDTH)
    v4 = v_all.reshape(-1, bsz, lp, WIDTH)
    small3 = small.reshape(bsz, lp, LANES)
    c_init = (jnp.zeros((bsz, HEADS, 1), F32) if fox_cache is None
              else fox_cache[3][layer].reshape(bsz, HEADS, 1))
    logf_rows, c_rows, gc_rows, cols = _gates(small3, n_tok, c_init)
    logf = logf_rows[:, :, :n_tok].transpose(0, 2, 1)

    if fox_cache is None:
        oa = _fox_prompt(z3, k4, v4, layer, c_rows)
    else:
        cache_k, cache_v, cp_flat, _ = fox_cache
        oa = _fox_decode(z3, k4, v4, n_tok, cache_k, cache_v, layer, cp_flat, c_rows[:, :, :n_tok])
        oa = jnp.pad(oa, ((0, 0), (0, lp - n_tok), (0, 0)))
    ob, s_new = _gdn(z3, wts["conv_w"], conv0, cols, gc_rows, wts["norm_w"], s0)

    x_out = _out_block(oa.reshape(bsz * lp, WIDTH), ob.reshape(bsz * lp, WIDTH), z2d, x2d, p_all, layer,
                       wts["wfox"], wts["wgdn"], wts["wout"], wts["wplp"], wts["wplg"], wts["vec"], alpha)

    new_conv = z3[:, n_tok - (CONV_W - 1):n_tok, Z_QKV_B * WIDTH:Z_QKV_B * WIDTH + CONV_CH]
    return x_out, k_all, v_all, logf, new_conv, s_new


def kernel(x_prompt, x_sample, cache_fox_k, cache_fox_v, cache_fox_logf, state_gdn_conv, state_gdn, p_prompt, p_sample, w_in, fox_f_bias, gdn_conv_w, gdn_a_log, gdn_dt_bias, gdn_norm_w, w_out_fox, w_out_gdn, w_out, ln_g, ln_b, w_pl_proj, w_pl_gate, pl_norm_w):
    depth = w_in.shape[0]
    bp, seq, d = x_prompt.shape
    bs, dseq, _ = x_sample.shape
    past = cache_fox_k.shape[2]
    ple = p_prompt.shape[-1]
    assert d % WIDTH == 0 and dseq >= CONV_W - 1 and seq >= CONV_W - 1
    alpha = (2 * depth) ** 0.25

    o_f = 3 * WIDTH
    o_ga = o_f + HEADS
    o_qkv = o_ga + WIDTH
    o_a = o_qkv + CONV_CH
    o_gb = o_a + 2 * HEADS
    w_in_t = jnp.swapaxes(w_in, 1, 2).astype(BF16)
    w_big = jnp.concatenate([w_in_t[:, :WIDTH], w_in_t[:, o_ga:o_a], w_in_t[:, o_gb:],
                             w_in_t[:, WIDTH:o_f]], axis=1)
    w_small = jnp.concatenate([w_in[..., o_f:o_ga], w_in[..., o_a:o_gb],
                               jnp.zeros((depth, d, LANES - 3 * HEADS), F32)], axis=-1)
    w_small_hi = w_small.astype(BF16)
    w_small = jnp.stack([w_small_hi, (w_small - w_small_hi.astype(F32)).astype(BF16)], axis=1)
    zpad = jnp.zeros((depth, LANES - 2 * HEADS), F32)
    par = jnp.stack([jnp.concatenate([fox_f_bias, gdn_dt_bias, zpad], axis=-1),
                     jnp.concatenate([jnp.zeros_like(fox_f_bias), gdn_a_log, zpad], axis=-1)], axis=1)
    par = jnp.pad(par, ((0, 0), (0, SUBLANES - 2), (0, 0)))
    vec = jnp.pad(jnp.stack([ln_g, ln_b, pl_norm_w], axis=1), ((0, 0), (0, SUBLANES - 3), (0, 0)))
    wfox, wgdn, wout = w_out_fox.astype(BF16), w_out_gdn.astype(BF16), w_out.astype(BF16)
    wplp, wplg = w_pl_proj.astype(BF16), w_pl_gate.astype(BF16)

    def layer_weights(i):
        return dict(w_big=w_big, w_small=w_small, par=par, conv_w=gdn_conv_w[i],
                    norm_w=gdn_norm_w[i], wfox=wfox, wgdn=wgdn, wout=wout,
                    wplp=wplp, wplg=wplg, vec=vec)

    lp_s = -(-dseq // GDN_CHUNK) * GDN_CHUNK
    xs = jnp.pad(x_sample, ((0, 0), (0, lp_s - dseq), (0, 0))).reshape(bs * lp_s, d)
    ps_all = jnp.pad(p_sample, ((0, 0), (0, 0), (0, lp_s - dseq), (0, 0))).reshape(depth, bs * lp_s, ple)
    xp = x_prompt.reshape(bp * seq, d)
    pp_all = p_prompt.reshape(depth, bp * seq, ple)

    cp_rows = _cumsum_rows(cache_fox_logf.transpose(0, 1, 3, 2).reshape(depth * bs * HEADS, past))
    cp_rows = cp_rows.reshape(depth, bs, HEADS, past)
    cp_flat = cp_rows.transpose(0, 1, 3, 2).reshape(depth, bs, 1, past * HEADS)
    cp_last = cp_rows[..., past - 1:].reshape(depth, bs * HEADS, 1)

    conv0_p = jnp.zeros((bp, CONV_W - 1, CONV_CH), F32)
    s0_p = jnp.zeros((bp, HEADS, HEAD_DIM, HEAD_DIM), F32)
    kp = lax.empty((depth, bp * seq, WIDTH), F32)
    vp = lax.empty((depth, bp * seq, WIDTH), F32)
    ks = lax.empty((depth, bs * lp_s, WIDTH), F32)
    vs = lax.empty((depth, bs * lp_s, WIDTH), F32)
    outs_p, outs_s = [], []
    for i in range(depth):
        wts = layer_weights(i)
        xp, kp, vp, *rest_p = _layer(xp, bp, seq, pp_all, i, wts, conv0_p, s0_p, alpha, kp, vp)
        xs, ks, vs, *rest_s = _layer(xs, bs, dseq, ps_all, i, wts, state_gdn_conv[i], state_gdn[i], alpha,
                                     ks, vs, fox_cache=(cache_fox_k, cache_fox_v, cp_flat, cp_last))
        outs_p.append(rest_p)
        outs_s.append(rest_s)
    stack = lambda outs, j: jnp.stack([o[j] for o in outs])
    heads_p = lambda a: a.reshape(depth, bp, seq, HEADS, HEAD_DIM)
    heads_s = lambda a: a.reshape(depth, bs, lp_s, HEADS, HEAD_DIM)[:, :, :dseq]
    y_prompt = xp.reshape(bp, seq, d)
    y_sample = xs.reshape(bs, lp_s, d)[:, :dseq]
    return (y_prompt, y_sample,
            heads_p(kp), heads_p(vp), stack(outs_p, 0), stack(outs_p, 1), stack(outs_p, 2),
            heads_s(ks), heads_s(vs), stack(outs_s, 0), stack(outs_s, 1), stack(outs_s, 2))
```
